```python
import jax, jax.numpy as jnp
from jax import lax
import numpy as np

D_MODEL = 2048
BATCH = 2
SEQ = 8192
DEPTH = 2

CTX_LEN = 256
GRID_W = 64
Q_BLOCK = 128
ROPE_THETA = 10000.0
LN_EPS = 1e-6
RMS_EPS = 1e-6

A_HEADS = 4
A_DK = 64
A_DV = 128
A_CHUNK = 128
A_CONV = 3
B_HEADS = 8
B_KV = 2
B_HD = 128
C_HEADS = 4
C_Q_LORA = 512
C_KV_LORA = 256
C_NOPE = 128
C_ROPE = 64
C_VD = 128
N_GROUPS = 4
EXP_PER_GROUP = 4
N_EXPERTS = N_GROUPS * EXP_PER_GROUP
TOP_K = 2
D_EXPERT = 512

A_IN = 2 * A_HEADS * A_DK + A_HEADS * A_DV + 4 * A_HEADS + A_HEADS * A_DV
B_IN = B_HEADS * B_HD + 2 * B_KV * B_HD
C_IN = C_Q_LORA + C_KV_LORA + C_ROPE
IN_WIDTH = A_IN + B_IN + C_IN
MIX_WIDTH = A_HEADS * A_DV + B_HEADS * B_HD + C_HEADS * C_VD

kernel_name = 'hybrid_mlstm_gqa_mla_hmoe_dit_trunk'


def layer_norm(x, g, b):
    xf = x.astype(jnp.float32)
    mu = jnp.mean(xf, axis=-1, keepdims=True)
    var = jnp.mean(jnp.square(xf - mu), axis=-1, keepdims=True)
    return ((xf - mu) * lax.rsqrt(var + LN_EPS) * g + b).astype(x.dtype)


def rms_norm(x, g):
    xf = x.astype(jnp.float32)
    y = xf * lax.rsqrt(jnp.mean(jnp.square(xf), axis=-1, keepdims=True) + RMS_EPS)
    return (y * g).astype(x.dtype)


def axial_rope_tables(T, dim):
    rows = T // GRID_W
    row = jnp.repeat(jnp.arange(rows, dtype=jnp.float32), GRID_W)
    col = jnp.tile(jnp.arange(GRID_W, dtype=jnp.float32), rows)
    n_freq = dim // 4
    inv = ROPE_THETA ** (-jnp.arange(n_freq, dtype=jnp.float32) / n_freq)
    ang = jnp.concatenate([row[:, None] * inv, col[:, None] * inv], axis=-1)
    return jnp.cos(ang), jnp.sin(ang)


def apply_rope(x, rope):
    cos, sin = rope
    xf = x.astype(jnp.float32)
    half = x.shape[-1] // 2
    x1, x2 = xf[..., :half], xf[..., half:]
    cs, sn = cos[:, None, :], sin[:, None, :]
    return jnp.concatenate([x1 * cs - x2 * sn, x2 * cs + x1 * sn], axis=-1).astype(x.dtype)


def centred_dwconv(x, w, b):
    y = lax.conv_general_dilated(x, w[:, None, :].astype(x.dtype), window_strides=(1,), padding='SAME',
                                 dimension_numbers=('NWC', 'WIO', 'NWC'), feature_group_count=x.shape[-1])
    return y + b


def block_attention(q, k, v):
    Bn, T, Hk, G, d = q.shape
    nb = T // Q_BLOCK
    qb = jnp.moveaxis(q.reshape(Bn, nb, Q_BLOCK, Hk, G, d), 1, 0)
    scale = d ** -0.5

    def attend(qblk):
        s = jnp.einsum('bqhgd,bkhd->bhgqk', qblk, k).astype(jnp.float32) * scale
        p = jax.nn.softmax(s, axis=-1).astype(v.dtype)
        return jnp.einsum('bhgqk,bkhe->bqhge', p, v)

    o = lax.map(attend, qb)
    return jnp.moveaxis(o, 0, 1).reshape(Bn, T, Hk, G, v.shape[-1])


def mlstm_scan(q, k, v, ig, lf, state):
    Bn, H, T, _ = q.shape
    nc = T // A_CHUNK

    def chunks(a):
        return jnp.moveaxis(a.reshape(a.shape[:2] + (nc, A_CHUNK) + a.shape[3:]), 2, 0)

    upto = jnp.tril(jnp.ones((A_CHUNK, A_CHUNK), dtype=bool))

    def step(carry, xs):
        C, n, m = carry
        qc, kc, vc, ic, fc = xs
        b = jnp.cumsum(fc, axis=-1)
        g = b + m[..., None]
        d = jnp.where(upto, b[..., :, None] - b[..., None, :] + ic[..., None, :], -jnp.inf)
        mt = jnp.maximum(g, jnp.max(d, axis=-1))
        s = jnp.einsum('bhtd,bhsd->bhts', qc, kc) * jnp.exp(d - mt[..., None])
        w_prev = jnp.exp(g - mt)
        num = w_prev[..., None] * jnp.einsum('bhvd,bhtd->bhtv', C, qc) + jnp.einsum('bhts,bhsv->bhtv', s, vc)
        den = w_prev * jnp.einsum('bhd,bhtd->bht', n, qc) + jnp.sum(s, axis=-1)
        h = num / jnp.maximum(jnp.abs(den), jnp.exp(-mt))[..., None]
        b_last = b[..., -1]
        a = b_last[..., None] - b + ic
        m_new = jnp.maximum(b_last + m, jnp.max(a, axis=-1))
        w_k = jnp.exp(a - m_new[..., None])
        decay = jnp.exp(b_last + m - m_new)
        C_new = decay[..., None, None] * C + jnp.einsum('bhs,bhsv,bhsd->bhvd', w_k, vc, kc)
        n_new = decay[..., None] * n + jnp.einsum('bhs,bhsd->bhd', w_k, kc)
        return (C_new, n_new, m_new), h

    state, h = lax.scan(step, state, (chunks(q), chunks(k), chunks(v), chunks(ig), chunks(lf)))
    return state, jnp.moveaxis(h, 0, 2).reshape(Bn, H, T, -1)


def mlstm_group(z, zc, conv_w, conv_b, gate_b, norm_g, need_ctx):
    QK = 2 * A_HEADS * A_DK
    V = A_HEADS * A_DV
    G = 4 * A_HEADS

    def prep(zz):
        Bn, T, _ = zz.shape
        qk = jax.nn.silu(centred_dwconv(zz[..., :QK], conv_w, conv_b))
        q = qk[..., :QK // 2].reshape(Bn, T, A_HEADS, A_DK)
        k = qk[..., QK // 2:].reshape(Bn, T, A_HEADS, A_DK) * (A_DK ** -0.5)
        v = zz[..., QK:QK + V].reshape(Bn, T, A_HEADS, A_DV)
        gates = zz[..., QK + V:QK + V + G].reshape(Bn, T, 4, A_HEADS) + gate_b
        og = jax.nn.sigmoid(zz[..., QK + V + G:])
        ig_f, ig_b, fg_f, fg_b = jnp.transpose(gates.astype(jnp.float32), (2, 0, 3, 1))
        q, k, v = (jnp.moveaxis(a.astype(jnp.float32), 1, 2) for a in (q, k, v))
        fwd = (q, k, v, ig_f, jax.nn.log_sigmoid(fg_f))
        bwd = tuple(jnp.flip(a, axis=2) for a in (q, k, v, ig_b, jax.nn.log_sigmoid(fg_b)))
        return fwd, bwd, og

    fwd_c, bwd_c, og_c = prep(zc)
    fwd_l, bwd_l, og_l = prep(z)
    Bn = z.shape[0]
    zero = (jnp.zeros((Bn, A_HEADS, A_DV, A_DK), jnp.float32), jnp.zeros((Bn, A_HEADS, A_DK), jnp.float32),
            jnp.zeros((Bn, A_HEADS), jnp.float32))
    st_f, hc_f = mlstm_scan(*fwd_c, zero)
    st_b, hc_b = mlstm_scan(*bwd_c, zero)
    _, hl_f = mlstm_scan(*fwd_l, st_f)
    _, hl_b = mlstm_scan(*bwd_l, st_b)

    def finish(hf, hb_rev, og):
        hs = jnp.moveaxis(hf + jnp.flip(hb_rev, axis=2), 1, 2)
        mu = jnp.mean(hs, axis=-1, keepdims=True)
        var = jnp.mean(jnp.square(hs - mu), axis=-1, keepdims=True)
        hn = ((hs - mu) * lax.rsqrt(var + LN_EPS)).reshape(hs.shape[0], hs.shape[1], -1) * norm_g
        return (hn * og.astype(jnp.float32)).astype(og.dtype)

    y = finish(hl_f, hl_b, og_l)
    yc = finish(hc_f, hc_b, og_c) if need_ctx else None
    return y, yc


def gqa_group(z, zc, qn_g, kn_g, rope, need_ctx):
    QW = B_HEADS * B_HD
    KW = B_KV * B_HD

    def prep(zz, use_rope):
        Bn, T, _ = zz.shape
        q = rms_norm(zz[..., :QW].reshape(Bn, T, B_HEADS, B_HD), qn_g)
        k = rms_norm(zz[..., QW:QW + KW].reshape(Bn, T, B_KV, B_HD), kn_g)
        v = zz[..., QW + KW:].reshape(Bn, T, B_KV, B_HD)
        if use_rope:
            q = apply_rope(q, rope)
            k = apply_rope(k, rope)
        return q.reshape(Bn, T, B_KV, B_HEADS // B_KV, B_HD), k, v

    ql, kl, vl = prep(z, True)
    qc, kc, vc = prep(zc, False)
    Bn, T = z.shape[:2]
    y = block_attention(ql, jnp.concatenate([kc, kl], axis=1), jnp.concatenate([vc, vl], axis=1)).reshape(Bn, T, QW)
    yc = block_attention(qc, kc, vc).reshape(Bn, zc.shape[1], QW) if need_ctx else None
    return y, yc


def mla_group(z, zc, qa_g, w_uq, kva_g, w_ukv, rope, need_ctx):
    def prep(zz, use_rope):
        Bn, T, _ = zz.shape
        cq = rms_norm(zz[..., :C_Q_LORA], qa_g)
        q = (cq @ w_uq).reshape(Bn, T, C_HEADS, C_NOPE + C_ROPE)
        ckv = rms_norm(zz[..., C_Q_LORA:C_Q_LORA + C_KV_LORA], kva_g)
        kv = (ckv @ w_ukv).reshape(Bn, T, C_HEADS, C_NOPE + C_VD)
        q_nope, q_rope = q[..., :C_NOPE], q[..., C_NOPE:]
        k_nope, v = kv[..., :C_NOPE], kv[..., C_NOPE:]
        k_rope = zz[..., C_Q_LORA + C_KV_LORA:][:, :, None, :]
        if use_rope:
            q_rope = apply_rope(q_rope, rope)
            k_rope = apply_rope(k_rope, rope)
        qf = jnp.concatenate([q_nope, q_rope], axis=-1)[:, :, :, None, :]
        kf = jnp.concatenate([k_nope, jnp.broadcast_to(k_rope, (Bn, T, C_HEADS, C_ROPE))], axis=-1)
        return qf, kf, v

    ql, kl, vl = prep(z, True)
    qc, kc, vc = prep(zc, False)
    Bn, T = z.shape[:2]
    y = block_attention(ql, jnp.concatenate([kc, kl], axis=1), jnp.concatenate([vc, vl], axis=1)).reshape(Bn, T, C_HEADS * C_VD)
    yc = block_attention(qc, kc, vc).reshape(Bn, zc.shape[1], C_HEADS * C_VD) if need_ctx else None
    return y, yc


def mixing_sublayer(h, hc, w_in, a_conv_w, a_conv_b, a_gate_b, a_norm_g, b_qnorm_g, b_knorm_g,
                    c_qnorm_g, c_w_uq, c_kvnorm_g, c_w_ukv, w_out, rope_b, rope_c, need_ctx):
    z = h @ w_in
    zc = hc @ w_in
    sa, sb = A_IN, A_IN + B_IN
    ya, yac = mlstm_group(z[..., :sa], zc[..., :sa], a_conv_w, a_conv_b, a_gate_b, a_norm_g, need_ctx)
    yb, ybc = gqa_group(z[..., sa:sb], zc[..., sa:sb], b_qnorm_g, b_knorm_g, rope_b, need_ctx)
    ym, ymc = mla_group(z[..., sb:], zc[..., sb:], c_qnorm_g, c_w_uq, c_kvnorm_g, c_w_ukv, rope_c, need_ctx)
    y = jnp.concatenate([ya, yb, ym], axis=-1) @ w_out
    yc = jnp.concatenate([yac, ybc, ymc], axis=-1) @ w_out if need_ctx else None
    return y, yc


def hier_moe(t, w_rg, b_rg, w_re, b_re, e_gate, e_up, e_down):
    g_prob = jax.nn.softmax((t @ w_rg + b_rg).astype(jnp.float32), axis=-1)
    g_idx = jnp.argmax(g_prob, axis=-1)
    g_w = jnp.take_along_axis(g_prob, g_idx[:, None], axis=-1)
    e_logits = (t @ w_re + b_re).astype(jnp.float32).reshape(-1, N_GROUPS, EXP_PER_GROUP)
    e_in = jnp.take_along_axis(e_logits, g_idx[:, None, None], axis=1)[:, 0]
    top_v, top_i = lax.top_k(jax.nn.softmax(e_in, axis=-1), TOP_K)
    weights = g_w * top_v / jnp.sum(top_v, axis=-1, keepdims=True)
    expert_id = g_idx[:, None] * EXP_PER_GROUP + top_i
    combine = jnp.sum(jax.nn.one_hot(expert_id, N_EXPERTS, dtype=jnp.float32) * weights[..., None], axis=1).astype(t.dtype)
    y = jnp.zeros_like(t)
    for e in range(N_EXPERTS):
        hid = jax.nn.silu(t @ e_gate[e]) * (t @ e_up[e])
        y = y + (combine[:, e:e + 1] * hid) @ e_down[e]
    return y


def setup_inputs(seed: int = 0) -> dict:
    key = jax.random.key(seed)
    ks = list(jax.random.split(key, 32))
    L = DEPTH
    beta = (8.0 * DEPTH) ** -0.25

    def nrm(shape, scale):
        return jax.random.normal(ks.pop(), shape, jnp.float32) * scale

    def gain(shape):
        return 1.0 + nrm(shape, 0.02)

    ig_b = nrm((L, 2, A_HEADS), 0.1)
    fg_b = jnp.linspace(3.0, 6.0, A_HEADS, dtype=jnp.float32)[None, None, :] + nrm((L, 2, A_HEADS), 0.1)
    return {
        'x': nrm((BATCH, SEQ, D_MODEL), 1.0),
        'c': nrm((BATCH, D_MODEL), 1.0),
        'ctx': nrm((BATCH, CTX_LEN, D_MODEL), 1.0),
        'c_ctx': nrm((D_MODEL,), 1.0),
        'w_mod': nrm((L, D_MODEL, 6 * D_MODEL), 0.5 * D_MODEL ** -0.5),
        'b_mod': nrm((L, 6 * D_MODEL), 0.02),
        'w_in': nrm((L, D_MODEL, IN_WIDTH), D_MODEL ** -0.5),
        'a_conv_w': nrm((L, A_CONV, 2 * A_HEADS * A_DK), A_CONV ** -0.5),
        'a_conv_b': nrm((L, 2 * A_HEADS * A_DK), 0.02),
        'a_gate_b': jnp.concatenate([ig_b, fg_b], axis=1),
        'a_norm_g': gain((L, A_HEADS * A_DV)),
        'b_qnorm_g': gain((L, B_HD)),
        'b_knorm_g': gain((L, B_HD)),
        'c_qnorm_g': gain((L, C_Q_LORA)),
        'c_w_uq': nrm((L, C_Q_LORA, C_HEADS * (C_NOPE + C_ROPE)), C_Q_LORA ** -0.5),
        'c_kvnorm_g': gain((L, C_KV_LORA)),
        'c_w_ukv': nrm((L, C_KV_LORA, C_HEADS * (C_NOPE + C_VD)), C_KV_LORA ** -0.5),
        'w_out': nrm((L, MIX_WIDTH, D_MODEL), beta * MIX_WIDTH ** -0.5),
        'ln1_g': gain((L, D_MODEL)),
        'ln1_b': nrm((L, D_MODEL), 0.02),
        'ln2_g': gain((L, D_MODEL)),
        'ln2_b': nrm((L, D_MODEL), 0.02),
        'w_rg': nrm((L, D_MODEL, N_GROUPS), D_MODEL ** -0.5),
        'b_rg': nrm((L, N_GROUPS), 0.01),
        'w_re': nrm((L, D_MODEL, N_EXPERTS), D_MODEL ** -0.5),
        'b_re': nrm((L, N_EXPERTS), 0.01),
        'e_gate': nrm((L, N_EXPERTS, D_MODEL, D_EXPERT), D_MODEL ** -0.5),
        'e_up': nrm((L, N_EXPERTS, D_MODEL, D_EXPERT), D_MODEL ** -0.5),
        'e_down': nrm((L, N_EXPERTS, D_EXPERT, D_MODEL), beta * D_EXPERT ** -0.5),
    }


def reference(x, c, ctx, c_ctx, w_mod, b_mod, w_in, a_conv_w, a_conv_b, a_gate_b, a_norm_g, b_qnorm_g, b_knorm_g,
              c_qnorm_g, c_w_uq, c_kvnorm_g, c_w_ukv, w_out, ln1_g, ln1_b, ln2_g, ln2_b, w_rg, b_rg, w_re, b_re,
              e_gate, e_up, e_down):
    alpha = (2.0 * DEPTH) ** 0.25
    T = x.shape[1]
    D = x.shape[-1]
    rope_b = axial_rope_tables(T, B_HD)
    rope_c = axial_rope_tables(T, C_ROPE)
    xc = ctx
    for l in range(DEPTH):
        need_ctx = l < DEPTH - 1
        mod = [m[:, None, :] for m in jnp.split(jax.nn.silu(c) @ w_mod[l] + b_mod[l], 6, axis=-1)]
        mod_c = jnp.split(jax.nn.silu(c_ctx) @ w_mod[l] + b_mod[l], 6, axis=-1)
        h = x * (1.0 + mod[1]) + mod[0]
        hc = xc * (1.0 + mod_c[1]) + mod_c[0]
        y, yc = mixing_sublayer(h, hc, w_in[l], a_conv_w[l], a_conv_b[l], a_gate_b[l], a_norm_g[l], b_qnorm_g[l],
                                b_knorm_g[l], c_qnorm_g[l], c_w_uq[l], c_kvnorm_g[l], c_w_ukv[l], w_out[l],
                                rope_b, rope_c, need_ctx)
        x = layer_norm(alpha * x + mod[2] * y, ln1_g[l], ln1_b[l])
        h = x * (1.0 + mod[4]) + mod[3]
        n_lat = h.shape[0] * h.shape[1]
        if need_ctx:
            xc = layer_norm(alpha * xc + mod_c[2] * yc, ln1_g[l], ln1_b[l])
            hc = xc * (1.0 + mod_c[4]) + mod_c[3]
            tokens = jnp.concatenate([h.reshape(-1, D), hc.reshape(-1, D)], axis=0)
        else:
            tokens = h.reshape(-1, D)
        f = hier_moe(tokens, w_rg[l], b_rg[l], w_re[l], b_re[l], e_gate[l], e_up[l], e_down[l])
        x = layer_norm(alpha * x + mod[5] * f[:n_lat].reshape(x.shape), ln2_g[l], ln2_b[l])
        if need_ctx:
            xc = layer_norm(alpha * xc + mod_c[5] * f[n_lat:].reshape(xc.shape), ln2_g[l], ln2_b[l])
    return x
```

```python
import functools

import jax
import jax.numpy as jnp
from jax import lax
from jax.experimental import pallas as pl
from jax.experimental.pallas import tpu as pltpu

F32 = jnp.float32
BF16 = jnp.bfloat16

GRID_W = 64
ROPE_THETA = 10000.0
LN_EPS = 1e-6
RMS_EPS = 1e-6
A_HEADS, A_DK, A_DV, A_CHUNK = 4, 64, 128, 128
B_HEADS, B_KV, B_HD = 8, 2, 128
C_HEADS, C_Q_LORA, C_KV_LORA, C_NOPE, C_ROPE, C_VD = 4, 512, 256, 128, 64, 128
N_GROUPS, EXP_PER_GROUP, TOP_K = 4, 4, 2

LANES = 128
ROW_TILE = 256
VMEM_LIMIT = 48 * 1024 * 1024
MOD_ROWS = 8

A_QK = 2 * A_HEADS * A_DK
A_V = A_HEADS * A_DV
A_G = 4 * A_HEADS
B_Q = B_HEADS * B_HD
B_K = B_KV * B_HD
SEG_WIDTHS = (A_QK, A_V, LANES, A_V, B_Q, B_K, B_K, C_Q_LORA, C_KV_LORA, LANES)
SEG_OFFS = tuple(sum(SEG_WIDTHS[:i]) for i in range(len(SEG_WIDTHS) + 1))
C_QPAD = 2 * LANES


def _cparams(sem):
    return pltpu.CompilerParams(dimension_semantics=sem, vmem_limit_bytes=VMEM_LIMIT)


def _sigmoid(x):
    return 1.0 / (1.0 + jnp.exp(-x))


def _pick_tile(n, cap):
    t = min(n, cap)
    while n % t or t % LANES:
        t -= LANES
    return t


def _mod_kernel(c_ref, w_ref, b_ref, o_ref):
    c = c_ref[...]
    s = (c * _sigmoid(c)).astype(BF16)
    o_ref[0] = jnp.dot(s, w_ref[0].astype(BF16), preferred_element_type=F32) + b_ref[0]


def _modulation(cvec, w_mod, b_mod):
    L, D, N = w_mod.shape
    tn = _pick_tile(N, 1024)
    return pl.pallas_call(
        _mod_kernel,
        grid=(L, N // tn),
        in_specs=[
            pl.BlockSpec((MOD_ROWS, D), lambda l, j: (0, 0)),
            pl.BlockSpec((1, D, tn), lambda l, j: (l, 0, j)),
            pl.BlockSpec((1, 1, tn), lambda l, j: (l, 0, j)),
        ],
        out_specs=pl.BlockSpec((1, MOD_ROWS, tn), lambda l, j: (l, 0, j)),
        out_shape=jax.ShapeDtypeStruct((L, MOD_ROWS, N), F32),
        compiler_params=_cparams(("arbitrary", "arbitrary")),
        name="modulation",
    )(cvec, w_mod, b_mod.reshape(L, 1, N))


def _rms(x, g):
    ms = jnp.mean(x * x, axis=-1, keepdims=True)
    return x * lax.rsqrt(ms + RMS_EPS) * g


def _in_kernel(x_ref, sh_ref, sc_ref, w_ref, cos_ref, sin_ref, qg_ref, kg_ref,
               aqk_ref, av_ref, ag_ref, aog_ref, bq_ref, bk_ref, bv_ref, cq_ref, ckv_ref, ckr_ref):
    xm = (x_ref[...] * (1.0 + sc_ref[0]) + sh_ref[0]).astype(BF16)

    def seg(i):
        return jnp.dot(xm, w_ref[:, SEG_OFFS[i]:SEG_OFFS[i + 1]], preferred_element_type=F32)

    aqk_ref[...] = seg(0)
    av_ref[...] = seg(1)
    ag_ref[...] = seg(2)
    aog_ref[...] = seg(3)
    cs = cos_ref[...]
    sn = sin_ref[...]

    def norm_rope(z, g, scale):
        y = _rms(z, g)
        return ((y * cs + pltpu.roll(y, B_HD // 2, 1) * sn) * scale).astype(BF16)

    zq = seg(4)
    for h in range(B_HEADS):
        bq_ref[h] = norm_rope(zq[:, h * B_HD:(h + 1) * B_HD], qg_ref[...], B_HD ** -0.5)
    zk = seg(5)
    for h in range(B_KV):
        bk_ref[h] = norm_rope(zk[:, h * B_HD:(h + 1) * B_HD], kg_ref[...], 1.0)
    zv = seg(6)
    for h in range(B_KV):
        bv_ref[h] = zv[:, h * B_HD:(h + 1) * B_HD].astype(BF16)
    cq_ref[...] = seg(7)
    ckv_ref[...] = seg(8)
    ckr_ref[...] = seg(9)


def _in_proj(xa, shift, scale, w_in_p, cos_b, sin_b, qg, kg, rowmod):
    R, D = xa.shape
    tm = ROW_TILE
    NW = w_in_p.shape[1]
    row = lambda i: (i, 0)
    fixed = lambda i: (0, 0)
    modspec = pl.BlockSpec((1, 1, D), lambda i: (rowmod(i), 0, 0))
    hspec = lambda nh: pl.BlockSpec((nh, tm, B_HD), lambda i: (0, i, 0))
    out_shape = (
        jax.ShapeDtypeStruct((R, A_QK), F32), jax.ShapeDtypeStruct((R, A_V), F32),
        jax.ShapeDtypeStruct((R, LANES), F32), jax.ShapeDtypeStruct((R, A_V), F32),
        jax.ShapeDtypeStruct((B_HEADS, R, B_HD), BF16), jax.ShapeDtypeStruct((B_KV, R, B_HD), BF16),
        jax.ShapeDtypeStruct((B_KV, R, B_HD), BF16),
        jax.ShapeDtypeStruct((R, C_Q_LORA), F32), jax.ShapeDtypeStruct((R, C_KV_LORA), F32),
        jax.ShapeDtypeStruct((R, LANES), F32),
    )
    out_specs = (
        pl.BlockSpec((tm, A_QK), row), pl.BlockSpec((tm, A_V), row), pl.BlockSpec((tm, LANES), row),
        pl.BlockSpec((tm, A_V), row), hspec(B_HEADS), hspec(B_KV), hspec(B_KV),
        pl.BlockSpec((tm, C_Q_LORA), row), pl.BlockSpec((tm, C_KV_LORA), row), pl.BlockSpec((tm, LANES), row),
    )
    return pl.pallas_call(
        _in_kernel,
        grid=(R // tm,),
        in_specs=[
            pl.BlockSpec((tm, D), row), modspec, modspec,
            pl.BlockSpec((D, NW), fixed),
            pl.BlockSpec((tm, LANES), row), pl.BlockSpec((tm, LANES), row),
            pl.BlockSpec((1, B_HD), fixed), pl.BlockSpec((1, B_HD), fixed),
        ],
        out_specs=out_specs,
        out_shape=out_shape,
        compiler_params=_cparams(("parallel",)),
        name="in_proj",
    )(xa, shift, scale, w_in_p, cos_b, sin_b, qg, kg)


def _conv_kernel(x_ref, p_ref, n_ref, w_ref, b_ref, q_ref, kt_ref, *, lat_rows, seq_len, ctx_len):
    tm = x_ref.shape[0]
    r0 = pl.program_id(0) * tm
    in_lat = r0 < lat_rows
    pos = jnp.where(in_lat, r0 % seq_len, (r0 - lat_rows) % ctx_len)
    slen = jnp.where(in_lat, seq_len, ctx_len)
    has_prev = (pos > 0).astype(F32)
    has_next = (pos + tm < slen).astype(F32)
    cur = x_ref[...]
    prev_row = p_ref[7:8, :] * has_prev
    next_row = n_ref[0:1, :] * has_next
    ridx = lax.broadcasted_iota(jnp.int32, cur.shape, 0)
    up = jnp.where(ridx == 0, prev_row, pltpu.roll(cur, 1, 0))
    dn = jnp.where(ridx == tm - 1, next_row, pltpu.roll(cur, tm - 1, 0))
    y = w_ref[0:1, :] * up + w_ref[1:2, :] * cur + w_ref[2:3, :] * dn + b_ref[...]
    s = y * _sigmoid(y)
    half = A_QK // 2
    q_ref[...] = s[:, :half].astype(BF16)
    kt_ref[...] = (s[:, half:] * (A_DK ** -0.5)).T.astype(BF16)


def _qk_conv(aqk, conv_w, conv_b, lat_rows, seq_len, ctx_len):
    R = aqk.shape[0]
    tm = ROW_TILE
    half = A_QK // 2
    sub = tm // 8
    nblk8 = R // 8
    kern = functools.partial(_conv_kernel, lat_rows=lat_rows, seq_len=seq_len, ctx_len=ctx_len)
    return pl.pallas_call(
        kern,
        grid=(R // tm,),
        in_specs=[
            pl.BlockSpec((tm, A_QK), lambda i: (i, 0)),
            pl.BlockSpec((8, A_QK), lambda i: (jnp.maximum(i * sub - 1, 0), 0)),
            pl.BlockSpec((8, A_QK), lambda i: (jnp.minimum((i + 1) * sub, nblk8 - 1), 0)),
            pl.BlockSpec((3, A_QK), lambda i: (0, 0)),
            pl.BlockSpec((1, A_QK), lambda i: (0, 0)),
        ],
        out_specs=(pl.BlockSpec((tm, half), lambda i: (i, 0)), pl.BlockSpec((half, tm), lambda i: (0, i))),
        out_shape=(jax.ShapeDtypeStruct((R, half), BF16), jax.ShapeDtypeStruct((half, R), BF16)),
        compiler_params=_cparams(("parallel",)),
        name="mlstm_conv",
    )(aqk, aqk, aqk, conv_w, conv_b)


def _split3(x):
    hi = x.astype(BF16)
    r1 = x - hi.astype(F32)
    mid = r1.astype(BF16)
    lo = (r1 - mid.astype(F32)).astype(BF16)
    return lo, mid, hi


def _scan_kernel(qf_ref, kf_ref, vf_ref, gf_ref, qb_ref, kb_ref, vb_ref, gb_ref, gbias_ref,
                 hf_ref, hb_ref, st_ref, m_ref):
    Lc = A_CHUNK

    @pl.when(pl.program_id(1) == 0)
    def _():
        st_ref[...] = jnp.zeros_like(st_ref)
        m_ref[...] = jnp.zeros_like(m_ref)

    row = lax.broadcasted_iota(jnp.int32, (Lc, Lc), 0)
    col = lax.broadcasted_iota(jnp.int32, (Lc, Lc), 1)
    lower = col <= row
    upper = col >= row
    lower_b = lower.astype(BF16)
    upper_b = upper.astype(BF16)
    lane = lax.broadcasted_iota(jnp.int32, (Lc, LANES), 1)
    ones_col = (lane == 0).astype(F32)
    neg_inf = jnp.float32(-jnp.inf)

    streams = ((qf_ref, kf_ref, vf_ref, gf_ref, hf_ref), (qb_ref, kb_ref, vb_ref, gb_ref, hb_ref))
    for d, (q_ref, k_ref, v_ref, g_ref, h_ref) in enumerate(streams):
        allowed = lower if d == 0 else upper
        col_mat = lower_b if d == 0 else upper_b
        row_mat = upper_b if d == 0 else lower_b
        G = g_ref[...] + gbias_ref[...]
        LS = jnp.minimum(G, 0.0) - jnp.log(1.0 + jnp.exp(-jnp.abs(G)))
        GT = G.T
        LST = LS.T
        cum_c = sum(jnp.dot(col_mat, p, preferred_element_type=F32) for p in _split3(LS))
        cum_r = sum(jnp.dot(p, row_mat, preferred_element_type=F32) for p in _split3(LST))
        tot_row = Lc - 1 if d == 0 else 0
        q_all = q_ref[...]
        for h in range(A_HEADS):
            ij = d * A_HEADS + h
            fj = 2 * A_HEADS + d * A_HEADS + h
            bcol = cum_c[:, fj:fj + 1]
            brow = cum_r[fj:fj + 1, :]
            irow = GT[ij:ij + 1, :]
            icol = G[:, ij:ij + 1]
            total = cum_c[tot_row:tot_row + 1, fj:fj + 1]
            m_old = m_ref[ij][:, 0:1]
            g = bcol + m_old
            dmat = jnp.where(allowed, bcol - brow + irow, neg_inf)
            mt = jnp.maximum(g, jnp.max(dmat, axis=1, keepdims=True))
            decay_mat = jnp.exp(dmat - mt)
            pair = h // 2
            qpair = q_all[:, pair * LANES:(pair + 1) * LANES]
            lo = (h % 2) * A_DK
            lane_ok = (lane >= lo) & (lane < lo + A_DK)
            qh = jnp.where(lane_ok, qpair, jnp.zeros_like(qpair))
            kt_pair = k_ref[pair * LANES:(pair + 1) * LANES, :]
            qk = jnp.dot(qh, kt_pair, preferred_element_type=F32)
            smat = qk * decay_mat
            vh = v_ref[:, h * A_DV:(h + 1) * A_DV]
            v1 = jnp.concatenate([vh, ones_col], axis=1)
            st_pair = jnp.concatenate([st_ref[d, 2 * pair], st_ref[d, 2 * pair + 1]], axis=0)
            w_prev = jnp.exp(g - mt)
            nd = w_prev * jnp.dot(qh, st_pair.astype(BF16), preferred_element_type=F32) \
                + jnp.dot(smat.astype(BF16), v1.astype(BF16), preferred_element_type=F32)
            num = nd[:, :A_DV]
            den = nd[:, A_DV:A_DV + 1]
            h_ref[:, h * A_DV:(h + 1) * A_DV] = num / jnp.maximum(jnp.abs(den), jnp.exp(-mt))
            a = total - bcol + icol
            m_new = jnp.maximum(total + m_old, jnp.max(a, axis=0, keepdims=True))
            w_k = jnp.exp(a - m_new)
            decay = jnp.exp(total + m_old - m_new)
            xw = (v1 * w_k).astype(BF16)
            kt_h = k_ref[h * A_DK:(h + 1) * A_DK, :]
            st_ref[d, h] = decay * st_ref[d, h] + jnp.dot(kt_h, xw, preferred_element_type=F32)
            m_ref[ij] = jnp.broadcast_to(m_new, (1, LANES))


def _mlstm_scan(qa, kt, av, ag, gbias, n_batch, seq_len, ctx_len):
    R = qa.shape[0]
    Lc = A_CHUNK
    nlat, nctx = seq_len // Lc, ctx_len // Lc
    ctx_base = n_batch * nlat
    half = A_QK // 2

    def fwd(b, c):
        return jnp.where(c < nctx, ctx_base + b * nctx + c, b * nlat + (c - nctx))

    def bwd(b, c):
        return jnp.where(c < nctx, ctx_base + b * nctx + (nctx - 1 - c), b * nlat + (nlat - 1 - (c - nctx)))

    def specs(ix):
        return [
            pl.BlockSpec((Lc, half), lambda b, c: (ix(b, c), 0)),
            pl.BlockSpec((half, Lc), lambda b, c: (0, ix(b, c))),
            pl.BlockSpec((Lc, A_V), lambda b, c: (ix(b, c), 0)),
            pl.BlockSpec((Lc, LANES), lambda b, c: (ix(b, c), 0)),
        ]

    hspec = lambda ix: pl.BlockSpec((Lc, A_V), lambda b, c: (ix(b, c), 0))
    return pl.pallas_call(
        _scan_kernel,
        grid=(n_batch, nlat + nctx),
        in_specs=specs(fwd) + specs(bwd) + [pl.BlockSpec((1, LANES), lambda b, c: (0, 0))],
        out_specs=(hspec(fwd), hspec(bwd)),
        out_shape=(jax.ShapeDtypeStruct((R, A_V), F32), jax.ShapeDtypeStruct((R, A_V), F32)),
        scratch_shapes=[
            pltpu.VMEM((2, A_HEADS, A_DK, 2 * LANES), F32),
            pltpu.VMEM((2 * A_HEADS, 1, LANES), F32),
        ],
        compiler_params=_cparams(("arbitrary", "arbitrary")),
        name="mlstm_scan",
    )(qa, kt, av, ag, qa, kt, av, ag, gbias)


def _mla_prep_kernel(cq_ref, ckv_ref, ckr_ref, qg_ref, kvg_ref, wuq_ref, wukv_ref, cos_ref, sin_ref,
                     q_ref, k_ref, v_ref):
    cs = cos_ref[...]
    sn = sin_ref[...]
    lane = lax.broadcasted_iota(jnp.int32, cs.shape, 1)
    quarter = C_ROPE // 2

    def rope(x):
        sw = jnp.where(lane < quarter, pltpu.roll(x, LANES - quarter, 1), pltpu.roll(x, quarter, 1))
        return x * cs + sw * sn

    cq = _rms(cq_ref[...], qg_ref[...]).astype(BF16)
    q = jnp.dot(cq, wuq_ref[...], preferred_element_type=F32)
    ckv = _rms(ckv_ref[...], kvg_ref[...]).astype(BF16)
    kv = jnp.dot(ckv, wukv_ref[...], preferred_element_type=F32)
    kr = rope(ckr_ref[...])
    scale = (C_NOPE + C_ROPE) ** -0.5
    for h in range(C_HEADS):
        o = h * C_QPAD
        qh = jnp.concatenate([q[:, o:o + C_NOPE], rope(q[:, o + C_NOPE:o + C_QPAD])], axis=1)
        q_ref[h] = (qh * scale).astype(BF16)
        o2 = h * (C_NOPE + C_VD)
        k_ref[h] = jnp.concatenate([kv[:, o2:o2 + C_NOPE], kr], axis=1).astype(BF16)
        v_ref[h] = kv[:, o2 + C_NOPE:o2 + C_NOPE + C_VD].astype(BF16)


def _mla_prep(cq, ckv, ckr, qg, kvg, wuq_p, wukv, cos_c, sin_c):
    R = cq.shape[0]
    tm = ROW_TILE
    row = lambda i: (i, 0)
    fixed = lambda i: (0, 0)
    hspec = lambda w: pl.BlockSpec((C_HEADS, tm, w), lambda i: (0, i, 0))
    return pl.pallas_call(
        _mla_prep_kernel,
        grid=(R // tm,),
        in_specs=[
            pl.BlockSpec((tm, C_Q_LORA), row), pl.BlockSpec((tm, C_KV_LORA), row), pl.BlockSpec((tm, LANES), row),
            pl.BlockSpec((1, C_Q_LORA), fixed), pl.BlockSpec((1, C_KV_LORA), fixed),
            pl.BlockSpec(wuq_p.shape, fixed), pl.BlockSpec(wukv.shape, fixed),
            pl.BlockSpec((tm, LANES), row), pl.BlockSpec((tm, LANES), row),
        ],
        out_specs=(hspec(C_QPAD), hspec(C_QPAD), hspec(C_VD)),
        out_shape=(jax.ShapeDtypeStruct((C_HEADS, R, C_QPAD), BF16), jax.ShapeDtypeStruct((C_HEADS, R, C_QPAD), BF16),
                   jax.ShapeDtypeStruct((C_HEADS, R, C_VD), BF16)),
        compiler_params=_cparams(("parallel",)),
        name="mla_prep",
    )(cq, ckv, ckr, qg, kvg, wuq_p, wukv, cos_c, sin_c)


def _attn_kernel(*refs, n_group, has_lat, has_alias):
    refs = list(refs)
    if has_alias:
        refs.pop(0)
    if has_lat:
        q_ref, kc_ref, vc_ref, kl_ref, vl_ref, o_ref, m_s, l_s, acc_s = refs
    else:
        q_ref, kc_ref, vc_ref, o_ref, m_s, l_s, acc_s = refs
    G, tq, dq = q_ref.shape
    dv = vc_ref.shape[-1]
    j = pl.program_id(3)
    nk = pl.num_programs(3)
    q = q_ref[...].reshape(G * tq, dq)

    def scores(k):
        return lax.dot_general(q, k, (((1,), (1,)), ((), ())), preferred_element_type=F32)

    @pl.when(j == 0)
    def _():
        s = scores(kc_ref[0])
        m = jnp.max(s, axis=1, keepdims=True)
        p = jnp.exp(s - m)
        m_s[...] = m
        l_s[...] = jnp.sum(p, axis=1, keepdims=True)
        acc_s[...] = jnp.dot(p.astype(BF16), vc_ref[0], preferred_element_type=F32)

    if has_lat:
        @pl.when(j > 0)
        def _():
            s = scores(kl_ref[0])
            m_prev = m_s[...]
            m_new = jnp.maximum(m_prev, jnp.max(s, axis=1, keepdims=True))
            alpha = jnp.exp(m_prev - m_new)
            p = jnp.exp(s - m_new)
            m_s[...] = m_new
            l_s[...] = alpha * l_s[...] + jnp.sum(p, axis=1, keepdims=True)
            acc_s[...] = alpha * acc_s[...] + jnp.dot(p.astype(BF16), vl_ref[0], preferred_element_type=F32)

    @pl.when(j == nk - 1)
    def _():
        o = acc_s[...] / l_s[...]
        for g in range(G):
            o_ref[:, g * dv:(g + 1) * dv] = o[g * tq:(g + 1) * tq, :].astype(o_ref.dtype)


def _attention(q, k, v, *, n_batch, seq_len, ctx_len, ctx_queries, tq, tk, out_rows, prev_out=None):
    Hq, R, dq = q.shape
    Hkv = k.shape[0]
    dv = v.shape[-1]
    G = Hq // Hkv
    lat_rows = n_batch * seq_len
    ctx_blk0 = lat_rows // ctx_len
    if ctx_queries:
        nq, q0, nlat = ctx_len // tq, lat_rows // tq, 0
        qrows = lambda b, i: q0 + b * nq + i
    else:
        nq, nlat = seq_len // tq, seq_len // tk
        qrows = lambda b, i: b * nq + i
    in_specs = [
        pl.BlockSpec((G, tq, dq), lambda b, h, i, j: (h, qrows(b, i), 0)),
        pl.BlockSpec((1, ctx_len, dq), lambda b, h, i, j: (h, ctx_blk0 + b, 0)),
        pl.BlockSpec((1, ctx_len, dv), lambda b, h, i, j: (h, ctx_blk0 + b, 0)),
    ]
    args = [q, k, v]
    if nlat:
        lat = lambda b, h, i, j: (h, b * nlat + jnp.maximum(j - 1, 0), 0)
        in_specs += [pl.BlockSpec((1, tk, dq), lat), pl.BlockSpec((1, tk, dv), lat)]
        args += [k, v]
    aliases = {}
    if prev_out is not None:
        in_specs = [pl.BlockSpec(memory_space=pl.ANY)] + in_specs
        args = [prev_out] + args
        aliases = {0: 0}
    kern = functools.partial(_attn_kernel, n_group=G, has_lat=bool(nlat), has_alias=prev_out is not None)
    return pl.pallas_call(
        kern,
        grid=(n_batch, Hkv, nq, 1 + nlat),
        in_specs=in_specs,
        out_specs=pl.BlockSpec((tq, G * dv), lambda b, h, i, j: (qrows(b, i), h)),
        out_shape=jax.ShapeDtypeStruct((out_rows, Hq * dv), BF16),
        scratch_shapes=[pltpu.VMEM((G * tq, 1), F32), pltpu.VMEM((G * tq, 1), F32), pltpu.VMEM((G * tq, dv), F32)],
        input_output_aliases=aliases,
        compiler_params=_cparams(("parallel", "parallel", "parallel", "arbitrary")),
        name="attention",
    )(*args)


def _layer_norm(r, g, b):
    mu = jnp.mean(r, axis=-1, keepdims=True)
    rc = r - mu
    var = jnp.mean(rc * rc, axis=-1, keepdims=True)
    return rc * lax.rsqrt(var + LN_EPS) * g + b


def _out_kernel(x_ref, hf_ref, hb_ref, og_ref, ng_ref, yb_ref, ym_ref, w_ref, gate_ref, lg_ref, lb_ref,
                sh_ref, sc_ref, wrh_ref, wrl_ref, br_ref, x1_ref, h2_ref, lo_ref, *, alpha):
    hs = hf_ref[...] + hb_ref[...]
    og = _sigmoid(og_ref[...])
    ng = ng_ref[...]
    parts = []
    for h in range(A_HEADS):
        sl = slice(h * A_DV, (h + 1) * A_DV)
        hh = hs[:, sl]
        mu = jnp.mean(hh, axis=-1, keepdims=True)
        hc = hh - mu
        var = jnp.mean(hc * hc, axis=-1, keepdims=True)
        parts.append(hc * lax.rsqrt(var + LN_EPS) * ng[:, sl] * og[:, sl])
    ya = jnp.concatenate(parts, axis=1).astype(BF16)
    o1 = A_V
    o2 = A_V + B_Q
    y = jnp.dot(ya, w_ref[0:o1, :], preferred_element_type=F32)
    y += jnp.dot(yb_ref[...], w_ref[o1:o2, :], preferred_element_type=F32)
    y += jnp.dot(ym_ref[...], w_ref[o2:, :], preferred_element_type=F32)
    x1 = _layer_norm(alpha * x_ref[...] + gate_ref[0] * y, lg_ref[...], lb_ref[...])
    x1_ref[...] = x1
    h2 = x1 * (1.0 + sc_ref[0]) + sh_ref[0]
    h2_hi = h2.astype(BF16)
    h2_ref[...] = h2_hi
    h2_lo = (h2 - h2_hi.astype(F32)).astype(BF16)
    wrh = wrh_ref[...]
    lo_ref[...] = (jnp.dot(h2_lo, wrh, preferred_element_type=F32)
                   + jnp.dot(h2_hi, wrl_ref[...], preferred_element_type=F32)
                   + jnp.dot(h2_hi, wrh, preferred_element_type=F32) + br_ref[...])


def _out_proj(xa, hf, hb, aog, ng, yb, ym, w_out, gate, lg, lb, shift, scale, wr_hi, wr_lo, br, rowmod, rows, alpha):
    D = xa.shape[1]
    tm = ROW_TILE
    row = lambda i: (i, 0)
    fixed = lambda i: (0, 0)
    modspec = pl.BlockSpec((1, 1, D), lambda i: (rowmod(i), 0, 0))
    return pl.pallas_call(
        functools.partial(_out_kernel, alpha=alpha),
        grid=(rows // tm,),
        in_specs=[
            pl.BlockSpec((tm, D), row),
            pl.BlockSpec((tm, A_V), row), pl.BlockSpec((tm, A_V), row), pl.BlockSpec((tm, A_V), row),
            pl.BlockSpec((1, A_V), fixed),
            pl.BlockSpec((tm, B_Q), row), pl.BlockSpec((tm, C_HEADS * C_VD), row),
            pl.BlockSpec(w_out.shape, fixed),
            modspec, pl.BlockSpec((1, D), fixed), pl.BlockSpec((1, D), fixed), modspec, modspec,
            pl.BlockSpec((D, LANES), fixed), pl.BlockSpec((D, LANES), fixed), pl.BlockSpec((1, LANES), fixed),
        ],
        out_specs=(pl.BlockSpec((tm, D), row), pl.BlockSpec((tm, D), row), pl.BlockSpec((tm, LANES), row)),
        out_shape=(jax.ShapeDtypeStruct((rows, D), F32), jax.ShapeDtypeStruct((rows, D), BF16),
                   jax.ShapeDtypeStruct((rows, LANES), F32)),
        compiler_params=_cparams(("parallel",)),
        name="out_proj_ln",
    )(xa, hf, hb, aog, ng, yb, ym, w_out, gate, lg, lb, shift, scale, wr_hi, wr_lo, br)


def _moe_kernel(te_ref, nv_ref, x_ref, wt_ref, wg_ref, wu_ref, wd_ref, o_ref):
    i = pl.program_id(0)

    @pl.when(i < nv_ref[0])
    def _():
        x = x_ref[...]
        g = jnp.dot(x, wg_ref[0], preferred_element_type=F32)
        u = jnp.dot(x, wu_ref[0], preferred_element_type=F32)
        hid = (wt_ref[...] * (g * _sigmoid(g) * u)).astype(BF16)
        o_ref[...] = jnp.dot(hid, wd_ref[0], preferred_element_type=F32)

    @pl.when(i >= nv_ref[0])
    def _():
        o_ref[...] = jnp.zeros_like(o_ref)


def _moe(tile_expert, n_valid, xs, wts, eg, eu, ed):
    Rs, D = xs.shape
    tm = ROW_TILE
    DE = eg.shape[-1]
    grid_spec = pltpu.PrefetchScalarGridSpec(
        num_scalar_prefetch=2,
        grid=(Rs // tm,),
        in_specs=[
            pl.BlockSpec((tm, D), lambda i, te, nv: (i, 0)),
            pl.BlockSpec((tm, 1), lambda i, te, nv: (i, 0)),
            pl.BlockSpec((1, D, DE), lambda i, te, nv: (te[i], 0, 0)),
            pl.BlockSpec((1, D, DE), lambda i, te, nv: (te[i], 0, 0)),
            pl.BlockSpec((1, DE, D), lambda i, te, nv: (te[i], 0, 0)),
        ],
        out_specs=pl.BlockSpec((tm, D), lambda i, te, nv: (i, 0)),
    )
    return pl.pallas_call(
        _moe_kernel,
        grid_spec=grid_spec,
        out_shape=jax.ShapeDtypeStruct((Rs, D), F32),
        compiler_params=_cparams(("arbitrary",)),
        name="moe_experts",
    )(tile_expert, n_valid, xs, wts, eg, eu, ed)


def _route(logits, tm, n_experts):
    N = logits.shape[0]
    gl = logits[:, :N_GROUPS]
    el = logits[:, N_GROUPS:N_GROUPS + n_experts]
    g_prob = jax.nn.softmax(gl, axis=-1)
    g_idx = jnp.argmax(g_prob, axis=-1)
    g_w = jnp.take_along_axis(g_prob, g_idx[:, None], axis=-1)
    e_in = jnp.take_along_axis(el.reshape(-1, N_GROUPS, EXP_PER_GROUP), g_idx[:, None, None], axis=1)[:, 0]
    top_v, top_i = lax.top_k(jax.nn.softmax(e_in, axis=-1), TOP_K)
    weights = g_w * top_v / jnp.sum(top_v, axis=-1, keepdims=True)
    expert_id = (g_idx[:, None] * EXP_PER_GROUP + top_i).astype(jnp.int32)
    e_flat = expert_id.reshape(-1)
    w_flat = weights.reshape(-1)
    tok_flat = jnp.repeat(jnp.arange(N, dtype=jnp.int32), TOP_K)
    onehot = (e_flat[:, None] == jnp.arange(n_experts, dtype=jnp.int32)[None, :]).astype(jnp.int32)
    rank = jnp.sum((jnp.cumsum(onehot, axis=0) - onehot) * onehot, axis=1)
    counts = jnp.sum(onehot, axis=0)
    padded = ((counts + tm - 1) // tm) * tm
    ends = jnp.cumsum(padded)
    dest = (ends - padded)[e_flat] + rank
    n_rows = TOP_K * N + n_experts * tm
    src_tok = jnp.zeros((n_rows,), jnp.int32).at[dest].set(tok_flat)
    wts = jnp.zeros((n_rows,), F32).at[dest].set(w_flat)
    tile_start = jnp.arange(n_rows // tm, dtype=jnp.int32) * tm
    tile_expert = jnp.minimum(jnp.sum(tile_start[:, None] >= ends[None, :], axis=1), n_experts - 1).astype(jnp.int32)
    n_valid = (ends[-1:] // tm).astype(jnp.int32)
    return src_tok, wts.reshape(-1, 1), tile_expert, n_valid, dest.reshape(N, TOP_K)


def _ln2_kernel(x_ref, f_ref, gate_ref, g_ref, b_ref, o_ref, *, alpha):
    o_ref[...] = _layer_norm(alpha * x_ref[...] + gate_ref[0] * f_ref[...], g_ref[...], b_ref[...])


def _ln2(x1, f, gate, g, b, rowmod, rows, alpha):
    D = x1.shape[1]
    tm = ROW_TILE
    row = lambda i: (i, 0)
    fixed = lambda i: (0, 0)
    return pl.pallas_call(
        functools.partial(_ln2_kernel, alpha=alpha),
        grid=(rows // tm,),
        in_specs=[pl.BlockSpec((tm, D), row), pl.BlockSpec((tm, D), row),
                  pl.BlockSpec((1, 1, D), lambda i: (rowmod(i), 0, 0)),
                  pl.BlockSpec((1, D), fixed), pl.BlockSpec((1, D), fixed)],
        out_specs=pl.BlockSpec((tm, D), row),
        out_shape=jax.ShapeDtypeStruct((rows, D), F32),
        compiler_params=_cparams(("parallel",)),
        name="residual_ln2",
    )(x1, f, gate, g, b)


def _rope_angles(T, dim):
    rows = T // GRID_W
    r = jnp.repeat(jnp.arange(rows, dtype=F32), GRID_W)
    c = jnp.tile(jnp.arange(GRID_W, dtype=F32), rows)
    n_freq = dim // 4
    inv = ROPE_THETA ** (-jnp.arange(n_freq, dtype=F32) / n_freq)
    return jnp.concatenate([r[:, None] * inv, c[:, None] * inv], axis=-1)


def _rope_tables(T, dim, n_batch, ctx_rows):
    ang = _rope_angles(T, dim)
    cos, sin = jnp.cos(ang), jnp.sin(ang)
    pad = jnp.zeros((T, LANES - dim), F32)
    cos_l = jnp.concatenate([cos, cos, pad], axis=1)
    sin_l = jnp.concatenate([-sin, sin, pad], axis=1)
    cos_c = jnp.concatenate([jnp.ones((ctx_rows, dim), F32), jnp.zeros((ctx_rows, LANES - dim), F32)], axis=1)
    sin_c = jnp.zeros((ctx_rows, LANES), F32)
    return (jnp.concatenate([jnp.tile(cos_l, (n_batch, 1)), cos_c], axis=0),
            jnp.concatenate([jnp.tile(sin_l, (n_batch, 1)), sin_c], axis=0))


def _relayout_w_in(w):
    D = w.shape[0]
    a_in = A_QK + A_V + A_G + A_V
    b0 = a_in
    c0 = a_in + B_Q + 2 * B_K
    zeros = lambda n: jnp.zeros((D, n), w.dtype)
    cols = [
        w[:, 0:A_QK], w[:, A_QK:A_QK + A_V],
        w[:, A_QK + A_V:A_QK + A_V + A_G], zeros(LANES - A_G),
        w[:, A_QK + A_V + A_G:a_in],
        w[:, b0:b0 + B_Q], w[:, b0 + B_Q:b0 + B_Q + B_K], w[:, b0 + B_Q + B_K:c0],
        w[:, c0:c0 + C_Q_LORA], w[:, c0 + C_Q_LORA:c0 + C_Q_LORA + C_KV_LORA],
        w[:, c0 + C_Q_LORA + C_KV_LORA:], zeros(LANES - C_ROPE),
    ]
    return jnp.concatenate(cols, axis=1).astype(BF16)


def kernel(x, c, ctx, c_ctx, w_mod, b_mod, w_in, a_conv_w, a_conv_b, a_gate_b, a_norm_g, b_qnorm_g, b_knorm_g,
           c_qnorm_g, c_w_uq, c_kvnorm_g, c_w_ukv, w_out, ln1_g, ln1_b, ln2_g, ln2_b, w_rg, b_rg, w_re, b_re,
           e_gate, e_up, e_down):
    Bn, T, D = x.shape
    CL = ctx.shape[1]
    depth = w_mod.shape[0]
    n_experts = e_gate.shape[1]
    tm = ROW_TILE
    assert T % tm == 0 and CL % tm == 0 and T % CL == 0 and T % GRID_W == 0 and Bn + 1 <= MOD_ROWS
    alpha = (2.0 * depth) ** 0.25
    lat_rows = Bn * T
    ctx_rows = Bn * CL
    R = lat_rows + ctx_rows
    lat_tiles = lat_rows // tm
    tiles_per_seq = T // tm
    rowmod = lambda i: jnp.where(i < lat_tiles, i // tiles_per_seq, Bn)

    cvec = jnp.concatenate([c, c_ctx[None, :], jnp.zeros((MOD_ROWS - Bn - 1, D), F32)], axis=0)
    mods = _modulation(cvec, w_mod, b_mod).reshape(depth, MOD_ROWS, 6, 1, D).transpose(0, 2, 1, 3, 4)
    cos_b, sin_b = _rope_tables(T, B_HD, Bn, ctx_rows)
    cos_c, sin_c = _rope_tables(T, C_ROPE, Bn, ctx_rows)
    tq_b = min(256, CL)
    tq_c = min(512, T)
    tk = min(512, T)

    xa = jnp.concatenate([x.reshape(lat_rows, D), ctx.reshape(ctx_rows, D)], axis=0)
    for l in range(depth):
        need_ctx = l < depth - 1
        rows = R if need_ctx else lat_rows
        md = mods[l]
        w_in_p = _relayout_w_in(w_in[l])
        gbias = jnp.concatenate([a_gate_b[l].reshape(1, A_G), jnp.zeros((1, LANES - A_G), F32)], axis=1)
        wuq_p = jnp.pad(c_w_uq[l].reshape(C_Q_LORA, C_HEADS, C_NOPE + C_ROPE),
                        ((0, 0), (0, 0), (0, C_QPAD - C_NOPE - C_ROPE))).reshape(C_Q_LORA, C_HEADS * C_QPAD).astype(BF16)
        wr = jnp.concatenate([w_rg[l], w_re[l], jnp.zeros((D, LANES - N_GROUPS - n_experts), F32)], axis=1)
        wr_hi = wr.astype(BF16)
        wr_lo = (wr - wr_hi.astype(F32)).astype(BF16)
        br = jnp.concatenate([b_rg[l], b_re[l], jnp.zeros((LANES - N_GROUPS - n_experts,), F32)])[None, :]

        aqk, av, ag, aog, bq, bk, bv, cq, ckv, ckr = _in_proj(
            xa, md[0], md[1], w_in_p, cos_b, sin_b, b_qnorm_g[l][None, :], b_knorm_g[l][None, :], rowmod)
        qa, kt = _qk_conv(aqk, a_conv_w[l], a_conv_b[l][None, :], lat_rows, T, CL)
        hf, hb = _mlstm_scan(qa, kt, av, ag, gbias, Bn, T, CL)
        qm, km, vm = _mla_prep(cq, ckv, ckr, c_qnorm_g[l][None, :], c_kvnorm_g[l][None, :], wuq_p,
                               c_w_ukv[l].astype(BF16), cos_c, sin_c)
        geo = dict(n_batch=Bn, seq_len=T, ctx_len=CL, out_rows=rows)
        yb = _attention(bq, bk, bv, ctx_queries=False, tq=tq_b, tk=tk, **geo)
        ym = _attention(qm, km, vm, ctx_queries=False, tq=tq_c, tk=tk, **geo)
        if need_ctx:
            yb = _attention(bq, bk, bv, ctx_queries=True, tq=tq_b, tk=tk, prev_out=yb, **geo)
            ym = _attention(qm, km, vm, ctx_queries=True, tq=min(tq_c, CL), tk=tk, prev_out=ym, **geo)
        x1, h2, logits = _out_proj(
            xa, hf, hb, aog, a_norm_g[l][None, :], yb, ym, w_out[l].astype(BF16), md[2],
            ln1_g[l][None, :], ln1_b[l][None, :], md[3], md[4], wr_hi, wr_lo, br, rowmod, rows, alpha)

        src_tok, wts, tile_expert, n_valid, dest = _route(logits, tm, n_experts)
        xs = jnp.take(h2, src_tok, axis=0)
        ys = _moe(tile_expert, n_valid, xs, wts, e_gate[l].astype(BF16), e_up[l].astype(BF16), e_down[l].astype(BF16))
        f = jnp.take(ys, dest[:, 0], axis=0) + jnp.take(ys, dest[:, 1], axis=0)
        xa = _ln2(x1, f, md[5], ln2_g[l][None, :], ln2_b[l][None, :], rowmod, rows, alpha)
    return xa[:lat_rows].reshape(Bn, T, D)
```

```python
import functools

import jax
import jax.numpy as jnp
from jax import lax
from jax.experimental import pallas as pl
from jax.experimental.pallas import tpu as pltpu

F32 = jnp.float32
BF16 = jnp.bfloat16

GRID_W = 64
ROPE_THETA = 10000.0
LN_EPS = 1e-6
RMS_EPS = 1e-6
A_HEADS, A_DK, A_DV, A_CHUNK = 4, 64, 128, 128
B_HEADS, B_KV, B_HD = 8, 2, 128
C_HEADS, C_Q_LORA, C_KV_LORA, C_NOPE, C_ROPE, C_VD = 4, 512, 256, 128, 64, 128
N_GROUPS, EXP_PER_GROUP, TOP_K = 4, 4, 2

LANES = 128
ROW_TILE = 256
VMEM_LIMIT = 48 * 1024 * 1024
MOD_ROWS = 8
LOG2E = 1.4426950408889634
ATTN_CHUNK_Q = 512
ATTN_SUB_K = 1024
ATTN_BLOCK_K = 2048
ATTN_BLOCK_Q = 1024

A_QK = 2 * A_HEADS * A_DK
A_V = A_HEADS * A_DV
A_G = 4 * A_HEADS
B_Q = B_HEADS * B_HD
B_K = B_KV * B_HD
SEG_WIDTHS = (A_QK, A_V, LANES, A_V, B_Q, B_K, B_K, C_Q_LORA, C_KV_LORA, LANES)
SEG_OFFS = tuple(sum(SEG_WIDTHS[:i]) for i in range(len(SEG_WIDTHS) + 1))
C_QPAD = 2 * LANES


def _cparams(sem):
    return pltpu.CompilerParams(dimension_semantics=sem, vmem_limit_bytes=VMEM_LIMIT)


def _sigmoid(x):
    return 1.0 / (1.0 + jnp.exp(-x))


def _pick_tile(n, cap):
    t = min(n, cap)
    while n % t or t % LANES:
        t -= LANES
    return t


def _mod_kernel(c_ref, w_ref, b_ref, o_ref):
    c = c_ref[...]
    s = (c * _sigmoid(c)).astype(BF16)
    o_ref[0] = jnp.dot(s, w_ref[0].astype(BF16), preferred_element_type=F32) + b_ref[0]


def _modulation(cvec, w_mod, b_mod):
    L, D, N = w_mod.shape
    tn = _pick_tile(N, 1024)
    return pl.pallas_call(
        _mod_kernel,
        grid=(L, N // tn),
        in_specs=[
            pl.BlockSpec((MOD_ROWS, D), lambda l, j: (0, 0)),
            pl.BlockSpec((1, D, tn), lambda l, j: (l, 0, j)),
            pl.BlockSpec((1, 1, tn), lambda l, j: (l, 0, j)),
        ],
        out_specs=pl.BlockSpec((1, MOD_ROWS, tn), lambda l, j: (l, 0, j)),
        out_shape=jax.ShapeDtypeStruct((L, MOD_ROWS, N), F32),
        compiler_params=_cparams(("arbitrary", "arbitrary")),
        name="modulation",
    )(cvec, w_mod, b_mod.reshape(L, 1, N))


def _rms(x, g):
    ms = jnp.mean(x * x, axis=-1, keepdims=True)
    return x * lax.rsqrt(ms + RMS_EPS) * g


def _in_kernel(x_ref, sh_ref, sc_ref, w_ref, cos_ref, sin_ref, qg_ref, kg_ref,
               aqk_ref, av_ref, ag_ref, aog_ref, bq_ref, bk_ref, bv_ref, cq_ref, ckv_ref, ckr_ref):
    xm = (x_ref[...] * (1.0 + sc_ref[0]) + sh_ref[0]).astype(BF16)

    def seg(i):
        return jnp.dot(xm, w_ref[:, SEG_OFFS[i]:SEG_OFFS[i + 1]], preferred_element_type=F32)

    aqk_ref[...] = seg(0)
    av_ref[...] = seg(1)
    ag_ref[...] = seg(2)
    aog_ref[...] = seg(3)
    cs = cos_ref[...]
    sn = sin_ref[...]

    def norm_rope(z, g, scale):
        y = _rms(z, g)
        return ((y * cs + pltpu.roll(y, B_HD // 2, 1) * sn) * scale).astype(BF16)

    zq = seg(4)
    for h in range(B_HEADS):
        bq_ref[h] = norm_rope(zq[:, h * B_HD:(h + 1) * B_HD], qg_ref[...], B_HD ** -0.5 * LOG2E)
    zk = seg(5)
    for h in range(B_KV):
        bk_ref[h] = norm_rope(zk[:, h * B_HD:(h + 1) * B_HD], kg_ref[...], 1.0)
    zv = seg(6)
    for h in range(B_KV):
        bv_ref[h] = zv[:, h * B_HD:(h + 1) * B_HD].T.astype(BF16)
    cq_ref[...] = seg(7)
    ckv_ref[...] = seg(8)
    ckr_ref[...] = seg(9)


def _in_proj(xa, shift, scale, w_in_p, cos_b, sin_b, qg, kg, rowmod):
    R, D = xa.shape
    tm = ROW_TILE
    NW = w_in_p.shape[1]
    row = lambda i: (i, 0)
    fixed = lambda i: (0, 0)
    modspec = pl.BlockSpec((1, 1, D), lambda i: (rowmod(i), 0, 0))
    hspec = lambda nh: pl.BlockSpec((nh, tm, B_HD), lambda i: (0, i, 0))
    out_shape = (
        jax.ShapeDtypeStruct((R, A_QK), F32), jax.ShapeDtypeStruct((R, A_V), F32),
        jax.ShapeDtypeStruct((R, LANES), F32), jax.ShapeDtypeStruct((R, A_V), F32),
        jax.ShapeDtypeStruct((B_HEADS, R, B_HD), BF16), jax.ShapeDtypeStruct((B_KV, R, B_HD), BF16),
        jax.ShapeDtypeStruct((B_KV, B_HD, R), BF16),
        jax.ShapeDtypeStruct((R, C_Q_LORA), F32), jax.ShapeDtypeStruct((R, C_KV_LORA), F32),
        jax.ShapeDtypeStruct((R, LANES), F32),
    )
    out_specs = (
        pl.BlockSpec((tm, A_QK), row), pl.BlockSpec((tm, A_V), row), pl.BlockSpec((tm, LANES), row),
        pl.BlockSpec((tm, A_V), row), hspec(B_HEADS), hspec(B_KV),
        pl.BlockSpec((B_KV, B_HD, tm), lambda i: (0, 0, i)),
        pl.BlockSpec((tm, C_Q_LORA), row), pl.BlockSpec((tm, C_KV_LORA), row), pl.BlockSpec((tm, LANES), row),
    )
    return pl.pallas_call(
        _in_kernel,
        grid=(R // tm,),
        in_specs=[
            pl.BlockSpec((tm, D), row), modspec, modspec,
            pl.BlockSpec((D, NW), fixed),
            pl.BlockSpec((tm, LANES), row), pl.BlockSpec((tm, LANES), row),
            pl.BlockSpec((1, B_HD), fixed), pl.BlockSpec((1, B_HD), fixed),
        ],
        out_specs=out_specs,
        out_shape=out_shape,
        compiler_params=_cparams(("parallel",)),
        name="in_proj",
    )(xa, shift, scale, w_in_p, cos_b, sin_b, qg, kg)


def _conv_kernel(x_ref, p_ref, n_ref, w_ref, b_ref, q_ref, kt_ref, *, lat_rows, seq_len, ctx_len):
    tm = x_ref.shape[0]
    r0 = pl.program_id(0) * tm
    in_lat = r0 < lat_rows
    pos = jnp.where(in_lat, r0 % seq_len, (r0 - lat_rows) % ctx_len)
    slen = jnp.where(in_lat, seq_len, ctx_len)
    has_prev = (pos > 0).astype(F32)
    has_next = (pos + tm < slen).astype(F32)
    cur = x_ref[...]
    prev_row = p_ref[7:8, :] * has_prev
    next_row = n_ref[0:1, :] * has_next
    ridx = lax.broadcasted_iota(jnp.int32, cur.shape, 0)
    up = jnp.where(ridx == 0, prev_row, pltpu.roll(cur, 1, 0))
    dn = jnp.where(ridx == tm - 1, next_row, pltpu.roll(cur, tm - 1, 0))
    y = w_ref[0:1, :] * up + w_ref[1:2, :] * cur + w_ref[2:3, :] * dn + b_ref[...]
    s = y * _sigmoid(y)
    half = A_QK // 2
    q_ref[...] = s[:, :half].astype(BF16)
    kt_ref[...] = (s[:, half:] * (A_DK ** -0.5)).T.astype(BF16)


def _qk_conv(aqk, conv_w, conv_b, lat_rows, seq_len, ctx_len):
    R = aqk.shape[0]
    tm = ROW_TILE
    half = A_QK // 2
    sub = tm // 8
    nblk8 = R // 8
    kern = functools.partial(_conv_kernel, lat_rows=lat_rows, seq_len=seq_len, ctx_len=ctx_len)
    return pl.pallas_call(
        kern,
        grid=(R // tm,),
        in_specs=[
            pl.BlockSpec((tm, A_QK), lambda i: (i, 0)),
            pl.BlockSpec((8, A_QK), lambda i: (jnp.maximum(i * sub - 1, 0), 0)),
            pl.BlockSpec((8, A_QK), lambda i: (jnp.minimum((i + 1) * sub, nblk8 - 1), 0)),
            pl.BlockSpec((3, A_QK), lambda i: (0, 0)),
            pl.BlockSpec((1, A_QK), lambda i: (0, 0)),
        ],
        out_specs=(pl.BlockSpec((tm, half), lambda i: (i, 0)), pl.BlockSpec((half, tm), lambda i: (0, i))),
        out_shape=(jax.ShapeDtypeStruct((R, half), BF16), jax.ShapeDtypeStruct((half, R), BF16)),
        compiler_params=_cparams(("parallel",)),
        name="mlstm_conv",
    )(aqk, aqk, aqk, conv_w, conv_b)


def _split3(x):
    hi = x.astype(BF16)
    r1 = x - hi.astype(F32)
    mid = r1.astype(BF16)
    lo = (r1 - mid.astype(F32)).astype(BF16)
    return lo, mid, hi


def _scan_kernel(qf_ref, kf_ref, vf_ref, gf_ref, qb_ref, kb_ref, vb_ref, gb_ref, gbias_ref,
                 hf_ref, hb_ref, st_ref, m_ref):
    Lc = A_CHUNK

    @pl.when(pl.program_id(1) == 0)
    def _():
        st_ref[...] = jnp.zeros_like(st_ref)
        m_ref[...] = jnp.zeros_like(m_ref)

    row = lax.broadcasted_iota(jnp.int32, (Lc, Lc), 0)
    col = lax.broadcasted_iota(jnp.int32, (Lc, Lc), 1)
    lower = col <= row
    upper = col >= row
    lower_b = lower.astype(BF16)
    upper_b = upper.astype(BF16)
    lane = lax.broadcasted_iota(jnp.int32, (Lc, LANES), 1)
    ones_col = (lane == 0).astype(F32)
    neg_inf = jnp.float32(-jnp.inf)

    streams = ((qf_ref, kf_ref, vf_ref, gf_ref, hf_ref), (qb_ref, kb_ref, vb_ref, gb_ref, hb_ref))
    for d, (q_ref, k_ref, v_ref, g_ref, h_ref) in enumerate(streams):
        allowed = lower if d == 0 else upper
        col_mat = lower_b if d == 0 else upper_b
        row_mat = upper_b if d == 0 else lower_b
        G = g_ref[...] + gbias_ref[...]
        LS = jnp.minimum(G, 0.0) - jnp.log(1.0 + jnp.exp(-jnp.abs(G)))
        GT = G.T
        LST = LS.T
        cum_c = sum(jnp.dot(col_mat, p, preferred_element_type=F32) for p in _split3(LS))
        cum_r = sum(jnp.dot(p, row_mat, preferred_element_type=F32) for p in _split3(LST))
        tot_row = Lc - 1 if d == 0 else 0
        q_all = q_ref[...]
        for h in range(A_HEADS):
            ij = d * A_HEADS + h
            fj = 2 * A_HEADS + d * A_HEADS + h
            bcol = cum_c[:, fj:fj + 1]
            brow = cum_r[fj:fj + 1, :]
            irow = GT[ij:ij + 1, :]
            icol = G[:, ij:ij + 1]
            total = cum_c[tot_row:tot_row + 1, fj:fj + 1]
            m_old = m_ref[ij][:, 0:1]
            g = bcol + m_old
            dmat = jnp.where(allowed, bcol - brow + irow, neg_inf)
            mt = jnp.maximum(g, jnp.max(dmat, axis=1, keepdims=True))
            decay_mat = jnp.exp(dmat - mt)
            pair = h // 2
            qpair = q_all[:, pair * LANES:(pair + 1) * LANES]
            lo = (h % 2) * A_DK
            lane_ok = (lane >= lo) & (lane < lo + A_DK)
            qh = jnp.where(lane_ok, qpair, jnp.zeros_like(qpair))
            kt_pair = k_ref[pair * LANES:(pair + 1) * LANES, :]
            qk = jnp.dot(qh, kt_pair, preferred_element_type=F32)
            smat = qk * decay_mat
            vh = v_ref[:, h * A_DV:(h + 1) * A_DV]
            v1 = jnp.concatenate([vh, ones_col], axis=1)
            st_pair = jnp.concatenate([st_ref[d, 2 * pair], st_ref[d, 2 * pair + 1]], axis=0)
            w_prev = jnp.exp(g - mt)
            nd = w_prev * jnp.dot(qh, st_pair.astype(BF16), preferred_element_type=F32) \
                + jnp.dot(smat.astype(BF16), v1.astype(BF16), preferred_element_type=F32)
            num = nd[:, :A_DV]
            den = nd[:, A_DV:A_DV + 1]
            h_ref[:, h * A_DV:(h + 1) * A_DV] = num / jnp.maximum(jnp.abs(den), jnp.exp(-mt))
            a = total - bcol + icol
            m_new = jnp.maximum(total + m_old, jnp.max(a, axis=0, keepdims=True))
            w_k = jnp.exp(a - m_new)
            decay = jnp.exp(total + m_old - m_new)
            xw = (v1 * w_k).astype(BF16)
            kt_h = k_ref[h * A_DK:(h + 1) * A_DK, :]
            st_ref[d, h] = decay * st_ref[d, h] + jnp.dot(kt_h, xw, preferred_element_type=F32)
            m_ref[ij] = jnp.broadcast_to(m_new, (1, LANES))


def _mlstm_scan(qa, kt, av, ag, gbias, n_batch, seq_len, ctx_len):
    R = qa.shape[0]
    Lc = A_CHUNK
    nlat, nctx = seq_len // Lc, ctx_len // Lc
    ctx_base = n_batch * nlat
    half = A_QK // 2

    def fwd(b, c):
        return jnp.where(c < nctx, ctx_base + b * nctx + c, b * nlat + (c - nctx))

    def bwd(b, c):
        return jnp.where(c < nctx, ctx_base + b * nctx + (nctx - 1 - c), b * nlat + (nlat - 1 - (c - nctx)))

    def specs(ix):
        return [
            pl.BlockSpec((Lc, half), lambda b, c: (ix(b, c), 0)),
            pl.BlockSpec((half, Lc), lambda b, c: (0, ix(b, c))),
            pl.BlockSpec((Lc, A_V), lambda b, c: (ix(b, c), 0)),
            pl.BlockSpec((Lc, LANES), lambda b, c: (ix(b, c), 0)),
        ]

    hspec = lambda ix: pl.BlockSpec((Lc, A_V), lambda b, c: (ix(b, c), 0))
    return pl.pallas_call(
        _scan_kernel,
        grid=(n_batch, nlat + nctx),
        in_specs=specs(fwd) + specs(bwd) + [pl.BlockSpec((1, LANES), lambda b, c: (0, 0))],
        out_specs=(hspec(fwd), hspec(bwd)),
        out_shape=(jax.ShapeDtypeStruct((R, A_V), F32), jax.ShapeDtypeStruct((R, A_V), F32)),
        scratch_shapes=[
            pltpu.VMEM((2, A_HEADS, A_DK, 2 * LANES), F32),
            pltpu.VMEM((2 * A_HEADS, 1, LANES), F32),
        ],
        compiler_params=_cparams(("arbitrary", "arbitrary")),
        name="mlstm_scan",
    )(qa, kt, av, ag, qa, kt, av, ag, gbias)


def _mla_prep_kernel(cq_ref, ckv_ref, ckr_ref, qg_ref, kvg_ref, wuq_ref, wukv_ref, cos_ref, sin_ref,
                     q_ref, k_ref, v_ref):
    cs = cos_ref[...]
    sn = sin_ref[...]
    lane = lax.broadcasted_iota(jnp.int32, cs.shape, 1)
    quarter = C_ROPE // 2

    def rope(x):
        sw = jnp.where(lane < quarter, pltpu.roll(x, LANES - quarter, 1), pltpu.roll(x, quarter, 1))
        return x * cs + sw * sn

    cq = _rms(cq_ref[...], qg_ref[...]).astype(BF16)
    q = jnp.dot(cq, wuq_ref[...], preferred_element_type=F32)
    ckv = _rms(ckv_ref[...], kvg_ref[...]).astype(BF16)
    kv = jnp.dot(ckv, wukv_ref[...], preferred_element_type=F32)
    kr = rope(ckr_ref[...])
    scale = (C_NOPE + C_ROPE) ** -0.5 * LOG2E
    for h in range(C_HEADS):
        o = h * C_QPAD
        qh = jnp.concatenate([q[:, o:o + C_NOPE], rope(q[:, o + C_NOPE:o + C_QPAD])], axis=1)
        q_ref[h] = (qh * scale).astype(BF16)
        o2 = h * (C_NOPE + C_VD)
        k_ref[h] = jnp.concatenate([kv[:, o2:o2 + C_NOPE], kr], axis=1).astype(BF16)
        v_ref[h] = kv[:, o2 + C_NOPE:o2 + C_NOPE + C_VD].T.astype(BF16)


def _mla_prep(cq, ckv, ckr, qg, kvg, wuq_p, wukv, cos_c, sin_c):
    R = cq.shape[0]
    tm = ROW_TILE
    row = lambda i: (i, 0)
    fixed = lambda i: (0, 0)
    hspec = lambda w: pl.BlockSpec((C_HEADS, tm, w), lambda i: (0, i, 0))
    return pl.pallas_call(
        _mla_prep_kernel,
        grid=(R // tm,),
        in_specs=[
            pl.BlockSpec((tm, C_Q_LORA), row), pl.BlockSpec((tm, C_KV_LORA), row), pl.BlockSpec((tm, LANES), row),
            pl.BlockSpec((1, C_Q_LORA), fixed), pl.BlockSpec((1, C_KV_LORA), fixed),
            pl.BlockSpec(wuq_p.shape, fixed), pl.BlockSpec(wukv.shape, fixed),
            pl.BlockSpec((tm, LANES), row), pl.BlockSpec((tm, LANES), row),
        ],
        out_specs=(hspec(C_QPAD), hspec(C_QPAD), pl.BlockSpec((C_HEADS, C_VD, tm), lambda i: (0, 0, i))),
        out_shape=(jax.ShapeDtypeStruct((C_HEADS, R, C_QPAD), BF16), jax.ShapeDtypeStruct((C_HEADS, R, C_QPAD), BF16),
                   jax.ShapeDtypeStruct((C_HEADS, C_VD, R), BF16)),
        compiler_params=_cparams(("parallel",)),
        name="mla_prep",
    )(cq, ckv, ckr, qg, kvg, wuq_p, wukv, cos_c, sin_c)


def _attn_kernel(*refs, cq, ksub, has_lat, has_alias):
    refs = list(refs)
    if has_alias:
        refs.pop(0)
    if has_lat:
        q_ref, kc_ref, vc_ref, kl_ref, vl_ref, o_ref, m_s, l_s, acc_s, st_s, p_s = refs
    else:
        q_ref, kc_ref, vc_ref, o_ref, m_s, l_s, acc_s, st_s, p_s = refs
    G, tq, dq = q_ref.shape
    dv = vc_ref.shape[1]
    j = pl.program_id(3)
    nk = pl.num_programs(3)
    qall = q_ref[...].reshape(G * tq, dq)
    nchunk = G * tq // cq

    @pl.when(j == 0)
    def _():
        m_s[...] = jnp.full_like(m_s, -jnp.inf)
        l_s[...] = jnp.zeros_like(l_s)
        acc_s[...] = jnp.zeros_like(acc_s)

    def scores(k, c, u):
        n = k.shape[0]
        qc = qall[c * cq:(c + 1) * cq, :]
        st_s[u, 0:n, :] = lax.dot_general(k, qc, (((1,), (1,)), ((), ())), preferred_element_type=F32)

    def absorb(vt, c, u):
        n = vt.shape[1]
        m_prev = m_s[c]
        m_new = jnp.maximum(m_prev, jnp.max(st_s[u, 0:n, :], axis=0, keepdims=True))
        alpha = jnp.exp2(m_prev - m_new)
        p = jnp.exp2(st_s[u, 0:n, :] - m_new)
        p_s[u, 0:n, :] = p.astype(BF16)
        m_s[c] = m_new
        l_s[c] = alpha * l_s[c] + jnp.sum(p, axis=0, keepdims=True)
        acc_s[c] = alpha * acc_s[c] + jnp.dot(vt, p_s[u, 0:n, :], preferred_element_type=F32)

    def run(units):
        scores(units[0][0], units[0][2], 0)
        for i, (_, vt, c) in enumerate(units):
            if i + 1 < len(units):
                scores(units[i + 1][0], units[i + 1][2], i + 1)
            absorb(vt, c, i)

    @pl.when(j == 0)
    def _():
        run([(kc_ref[0], vc_ref[0], c) for c in range(nchunk)])

    if has_lat:
        tk = kl_ref.shape[1]
        run([(kl_ref[0, s * ksub:(s + 1) * ksub, :], vl_ref[0, :, s * ksub:(s + 1) * ksub], c)
             for s in range(tk // ksub) for c in range(nchunk)])

    @pl.when(j == nk - 1)
    def _():
        piece = min(cq, tq)
        for c in range(nchunk):
            o = (acc_s[c] / l_s[c]).T
            for part in range(cq // piece):
                row0 = c * cq + part * piece
                g, r0 = row0 // tq, row0 % tq
                o_ref[r0:r0 + piece, g * dv:(g + 1) * dv] = o[part * piece:(part + 1) * piece, :].astype(o_ref.dtype)


def _attention(q, k, vt, *, n_batch, seq_len, ctx_len, ctx_queries, tq, tk, out_rows, prev_out=None):
    Hq, R, dq = q.shape
    Hkv = k.shape[0]
    dv = vt.shape[1]
    G = Hq // Hkv
    lat_rows = n_batch * seq_len
    ctx_blk0 = lat_rows // ctx_len
    if ctx_queries:
        nq, q0, nlat = ctx_len // tq, lat_rows // tq, 0
        qrows = lambda b, i: q0 + b * nq + i
    else:
        nq, nlat = seq_len // tq, seq_len // tk
        qrows = lambda b, i: b * nq + i
    in_specs = [
        pl.BlockSpec((G, tq, dq), lambda b, h, i, j: (h, qrows(b, i), 0)),
        pl.BlockSpec((1, ctx_len, dq), lambda b, h, i, j: (h, ctx_blk0 + b, 0)),
        pl.BlockSpec((1, dv, ctx_len), lambda b, h, i, j: (h, 0, ctx_blk0 + b)),
    ]
    args = [q, k, vt]
    if nlat:
        in_specs += [pl.BlockSpec((1, tk, dq), lambda b, h, i, j: (h, b * nlat + j, 0)),
                     pl.BlockSpec((1, dv, tk), lambda b, h, i, j: (h, 0, b * nlat + j))]
        args += [k, vt]
    aliases = {}
    if prev_out is not None:
        in_specs = [pl.BlockSpec(memory_space=pl.ANY)] + in_specs
        args = [prev_out] + args
        aliases = {0: 0}
    cq = min(ATTN_CHUNK_Q, G * tq)
    ksub = min(ATTN_SUB_K, tk)
    assert ctx_len <= ksub and tk % ksub == 0 and (G * tq) % cq == 0 and max(cq, tq) % min(cq, tq) == 0
    nchunk = G * tq // cq
    units = nchunk * max(tk // ksub if nlat else 1, 1)
    kern = functools.partial(_attn_kernel, cq=cq, ksub=ksub, has_lat=bool(nlat), has_alias=prev_out is not None)
    return pl.pallas_call(
        kern,
        grid=(n_batch, Hkv, nq, max(nlat, 1)),
        in_specs=in_specs,
        out_specs=pl.BlockSpec((tq, G * dv), lambda b, h, i, j: (qrows(b, i), h)),
        out_shape=jax.ShapeDtypeStruct((out_rows, Hq * dv), BF16),
        scratch_shapes=[pltpu.VMEM((nchunk, 1, cq), F32), pltpu.VMEM((nchunk, 1, cq), F32),
                        pltpu.VMEM((nchunk, dv, cq), F32),
                        pltpu.VMEM((units, ksub, cq), F32), pltpu.VMEM((units, ksub, cq), BF16)],
        input_output_aliases=aliases,
        compiler_params=_cparams(("parallel", "parallel", "parallel", "arbitrary")),
        name="attention",
    )(*args)


def _layer_norm(r, g, b):
    mu = jnp.mean(r, axis=-1, keepdims=True)
    rc = r - mu
    var = jnp.mean(rc * rc, axis=-1, keepdims=True)
    return rc * lax.rsqrt(var + LN_EPS) * g + b


def _out_kernel(x_ref, hf_ref, hb_ref, og_ref, ng_ref, yb_ref, ym_ref, w_ref, gate_ref, lg_ref, lb_ref,
                sh_ref, sc_ref, wrh_ref, wrl_ref, br_ref, x1_ref, h2_ref, lo_ref, *, alpha):
    hs = hf_ref[...] + hb_ref[...]
    og = _sigmoid(og_ref[...])
    ng = ng_ref[...]
    parts = []
    for h in range(A_HEADS):
        sl = slice(h * A_DV, (h + 1) * A_DV)
        hh = hs[:, sl]
        mu = jnp.mean(hh, axis=-1, keepdims=True)
        hc = hh - mu
        var = jnp.mean(hc * hc, axis=-1, keepdims=True)
        parts.append(hc * lax.rsqrt(var + LN_EPS) * ng[:, sl] * og[:, sl])
    ya = jnp.concatenate(parts, axis=1).astype(BF16)
    o1 = A_V
    o2 = A_V + B_Q
    y = jnp.dot(ya, w_ref[0:o1, :], preferred_element_type=F32)
    y += jnp.dot(yb_ref[...], w_ref[o1:o2, :], preferred_element_type=F32)
    y += jnp.dot(ym_ref[...], w_ref[o2:, :], preferred_element_type=F32)
    x1 = _layer_norm(alpha * x_ref[...] + gate_ref[0] * y, lg_ref[...], lb_ref[...])
    x1_ref[...] = x1
    h2 = x1 * (1.0 + sc_ref[0]) + sh_ref[0]
    h2_hi = h2.astype(BF16)
    h2_ref[...] = h2_hi
    h2_lo = (h2 - h2_hi.astype(F32)).astype(BF16)
    wrh = wrh_ref[...]
    lo_ref[...] = (jnp.dot(h2_lo, wrh, preferred_element_type=F32)
                   + jnp.dot(h2_hi, wrl_ref[...], preferred_element_type=F32)
                   + jnp.dot(h2_hi, wrh, preferred_element_type=F32) + br_ref[...])


def _out_proj(xa, hf, hb, aog, ng, yb, ym, w_out, gate, lg, lb, shift, scale, wr_hi, wr_lo, br, rowmod, rows, alpha):
    D = xa.shape[1]
    tm = ROW_TILE
    row = lambda i: (i, 0)
    fixed = lambda i: (0, 0)
    modspec = pl.BlockSpec((1, 1, D), lambda i: (rowmod(i), 0, 0))
    return pl.pallas_call(
        functools.partial(_out_kernel, alpha=alpha),
        grid=(rows // tm,),
        in_specs=[
            pl.BlockSpec((tm, D), row),
            pl.BlockSpec((tm, A_V), row), pl.BlockSpec((tm, A_V), row), pl.BlockSpec((tm, A_V), row),
            pl.BlockSpec((1, A_V), fixed),
            pl.BlockSpec((tm, B_Q), row), pl.BlockSpec((tm, C_HEADS * C_VD), row),
            pl.BlockSpec(w_out.shape, fixed),
            modspec, pl.BlockSpec((1, D), fixed), pl.BlockSpec((1, D), fixed), modspec, modspec,
            pl.BlockSpec((D, LANES), fixed), pl.BlockSpec((D, LANES), fixed), pl.BlockSpec((1, LANES), fixed),
        ],
        out_specs=(pl.BlockSpec((tm, D), row), pl.BlockSpec((tm, D), row), pl.BlockSpec((tm, LANES), row)),
        out_shape=(jax.ShapeDtypeStruct((rows, D), F32), jax.ShapeDtypeStruct((rows, D), BF16),
                   jax.ShapeDtypeStruct((rows, LANES), F32)),
        compiler_params=_cparams(("parallel",)),
        name="out_proj_ln",
    )(xa, hf, hb, aog, ng, yb, ym, w_out, gate, lg, lb, shift, scale, wr_hi, wr_lo, br)


def _moe_kernel(te_ref, nv_ref, x_ref, wt_ref, wg_ref, wu_ref, wd_ref, o_ref):
    i = pl.program_id(0)

    @pl.when(i < nv_ref[0])
    def _():
        x = x_ref[...]
        g = jnp.dot(x, wg_ref[0, 0].astype(BF16), preferred_element_type=F32)
        u = jnp.dot(x, wu_ref[0, 0].astype(BF16), preferred_element_type=F32)
        hid = (wt_ref[...] * (g * _sigmoid(g) * u)).astype(BF16)
        o_ref[...] = jnp.dot(hid, wd_ref[0, 0].astype(BF16), preferred_element_type=F32)

    @pl.when(i >= nv_ref[0])
    def _():
        o_ref[...] = jnp.zeros_like(o_ref)


def _moe(tile_expert, n_valid, xs, wts, eg, eu, ed, layer):
    Rs, D = xs.shape
    tm = ROW_TILE
    DE = eg.shape[-1]
    grid_spec = pltpu.PrefetchScalarGridSpec(
        num_scalar_prefetch=2,
        grid=(Rs // tm,),
        in_specs=[
            pl.BlockSpec((tm, D), lambda i, te, nv: (i, 0)),
            pl.BlockSpec((tm, 1), lambda i, te, nv: (i, 0)),
            pl.BlockSpec((1, 1, D, DE), lambda i, te, nv: (layer, te[i], 0, 0)),
            pl.BlockSpec((1, 1, D, DE), lambda i, te, nv: (layer, te[i], 0, 0)),
            pl.BlockSpec((1, 1, DE, D), lambda i, te, nv: (layer, te[i], 0, 0)),
        ],
        out_specs=pl.BlockSpec((tm, D), lambda i, te, nv: (i, 0)),
    )
    return pl.pallas_call(
        _moe_kernel,
        grid_spec=grid_spec,
        out_shape=jax.ShapeDtypeStruct((Rs, D), F32),
        compiler_params=_cparams(("arbitrary",)),
        name="moe_experts",
    )(tile_expert, n_valid, xs, wts, eg, eu, ed)


def _route(logits, tm, n_experts):
    N = logits.shape[0]
    gl = logits[:, :N_GROUPS]
    el = logits[:, N_GROUPS:N_GROUPS + n_experts]
    g_prob = jax.nn.softmax(gl, axis=-1)
    g_idx = jnp.argmax(g_prob, axis=-1)
    g_w = jnp.take_along_axis(g_prob, g_idx[:, None], axis=-1)
    e_in = jnp.take_along_axis(el.reshape(-1, N_GROUPS, EXP_PER_GROUP), g_idx[:, None, None], axis=1)[:, 0]
    top_v, top_i = lax.top_k(jax.nn.softmax(e_in, axis=-1), TOP_K)
    weights = g_w * top_v / jnp.sum(top_v, axis=-1, keepdims=True)
    expert_id = (g_idx[:, None] * EXP_PER_GROUP + top_i).astype(jnp.int32)
    e_flat = expert_id.reshape(-1)
    w_flat = weights.reshape(-1)
    n_flat = TOP_K * N
    order = jnp.argsort(e_flat, stable=True).astype(jnp.int32)
    inv = jnp.argsort(order).astype(jnp.int32)
    counts = jnp.sum((e_flat[:, None] == jnp.arange(n_experts, dtype=jnp.int32)[None, :]).astype(jnp.int32), axis=0)
    starts = jnp.cumsum(counts) - counts
    padded = ((counts + tm - 1) // tm) * tm
    ends = jnp.cumsum(padded)
    pstarts = ends - padded
    n_rows = n_flat + n_experts * tm
    tile_start = jnp.arange(n_rows // tm, dtype=jnp.int32) * tm
    tile_expert = jnp.minimum(jnp.sum(tile_start[:, None] >= ends[None, :], axis=1), n_experts - 1).astype(jnp.int32)
    n_valid = (ends[-1:] // tm).astype(jnp.int32)
    row = jnp.arange(n_rows, dtype=jnp.int32)
    row_e = jnp.repeat(tile_expert, tm)
    idx = row - pstarts[row_e]
    valid = idx < counts[row_e]
    flat = order[jnp.clip(starts[row_e] + idx, 0, n_flat - 1)]
    src_tok = jnp.where(valid, flat // TOP_K, 0)
    wts = jnp.where(valid, w_flat[flat], 0.0)
    dest = inv - starts[e_flat] + pstarts[e_flat]
    return src_tok, wts.reshape(-1, 1), tile_expert, n_valid, dest.reshape(N, TOP_K)


def _ln2_kernel(x_ref, f_ref, gate_ref, g_ref, b_ref, o_ref, *, alpha):
    o_ref[...] = _layer_norm(alpha * x_ref[...] + gate_ref[0] * f_ref[...], g_ref[...], b_ref[...])


def _ln2(x1, f, gate, g, b, rowmod, rows, alpha):
    D = x1.shape[1]
    tm = ROW_TILE
    row = lambda i: (i, 0)
    fixed = lambda i: (0, 0)
    return pl.pallas_call(
        functools.partial(_ln2_kernel, alpha=alpha),
        grid=(rows // tm,),
        in_specs=[pl.BlockSpec((tm, D), row), pl.BlockSpec((tm, D), row),
                  pl.BlockSpec((1, 1, D), lambda i: (rowmod(i), 0, 0)),
                  pl.BlockSpec((1, D), fixed), pl.BlockSpec((1, D), fixed)],
        out_specs=pl.BlockSpec((tm, D), row),
        out_shape=jax.ShapeDtypeStruct((rows, D), F32),
        compiler_params=_cparams(("parallel",)),
        name="residual_ln2",
    )(x1, f, gate, g, b)


def _rope_angles(T, dim):
    rows = T // GRID_W
    r = jnp.repeat(jnp.arange(rows, dtype=F32), GRID_W)
    c = jnp.tile(jnp.arange(GRID_W, dtype=F32), rows)
    n_freq = dim // 4
    inv = ROPE_THETA ** (-jnp.arange(n_freq, dtype=F32) / n_freq)
    return jnp.concatenate([r[:, None] * inv, c[:, None] * inv], axis=-1)


def _rope_tables(T, dim, n_batch, ctx_rows):
    ang = _rope_angles(T, dim)
    cos, sin = jnp.cos(ang), jnp.sin(ang)
    pad = jnp.zeros((T, LANES - dim), F32)
    cos_l = jnp.concatenate([cos, cos, pad], axis=1)
    sin_l = jnp.concatenate([-sin, sin, pad], axis=1)
    cos_c = jnp.concatenate([jnp.ones((ctx_rows, dim), F32), jnp.zeros((ctx_rows, LANES - dim), F32)], axis=1)
    sin_c = jnp.zeros((ctx_rows, LANES), F32)
    return (jnp.concatenate([jnp.tile(cos_l, (n_batch, 1)), cos_c], axis=0),
            jnp.concatenate([jnp.tile(sin_l, (n_batch, 1)), sin_c], axis=0))


def _relayout_w_in(w):
    D = w.shape[0]
    a_in = A_QK + A_V + A_G + A_V
    b0 = a_in
    c0 = a_in + B_Q + 2 * B_K
    zeros = lambda n: jnp.zeros((D, n), w.dtype)
    cols = [
        w[:, 0:A_QK], w[:, A_QK:A_QK + A_V],
        w[:, A_QK + A_V:A_QK + A_V + A_G], zeros(LANES - A_G),
        w[:, A_QK + A_V + A_G:a_in],
        w[:, b0:b0 + B_Q], w[:, b0 + B_Q:b0 + B_Q + B_K], w[:, b0 + B_Q + B_K:c0],
        w[:, c0:c0 + C_Q_LORA], w[:, c0 + C_Q_LORA:c0 + C_Q_LORA + C_KV_LORA],
        w[:, c0 + C_Q_LORA + C_KV_LORA:], zeros(LANES - C_ROPE),
    ]
    return jnp.concatenate(cols, axis=1).astype(BF16)


def kernel(x, c, ctx, c_ctx, w_mod, b_mod, w_in, a_conv_w, a_conv_b, a_gate_b, a_norm_g, b_qnorm_g, b_knorm_g,
           c_qnorm_g, c_w_uq, c_kvnorm_g, c_w_ukv, w_out, ln1_g, ln1_b, ln2_g, ln2_b, w_rg, b_rg, w_re, b_re,
           e_gate, e_up, e_down):
    Bn, T, D = x.shape
    CL = ctx.shape[1]
    depth = w_mod.shape[0]
    n_experts = e_gate.shape[1]
    tm = ROW_TILE
    assert T % tm == 0 and CL % tm == 0 and T % CL == 0 and T % GRID_W == 0 and Bn + 1 <= MOD_ROWS
    alpha = (2.0 * depth) ** 0.25
    lat_rows = Bn * T
    ctx_rows = Bn * CL
    R = lat_rows + ctx_rows
    lat_tiles = lat_rows // tm
    tiles_per_seq = T // tm
    rowmod = lambda i: jnp.where(i < lat_tiles, i // tiles_per_seq, Bn)

    cvec = jnp.concatenate([c, c_ctx[None, :], jnp.zeros((MOD_ROWS - Bn - 1, D), F32)], axis=0)
    mods = _modulation(cvec, w_mod, b_mod).reshape(depth, MOD_ROWS, 6, 1, D).transpose(0, 2, 1, 3, 4)
    cos_b, sin_b = _rope_tables(T, B_HD, Bn, ctx_rows)
    cos_c, sin_c = _rope_tables(T, C_ROPE, Bn, ctx_rows)
    tq_b = min(ATTN_BLOCK_Q // (B_HEADS // B_KV), CL)
    tq_c = min(ATTN_BLOCK_Q, T)
    tk = min(ATTN_BLOCK_K, T)

    xa = jnp.concatenate([x.reshape(lat_rows, D), ctx.reshape(ctx_rows, D)], axis=0)
    for l in range(depth):
        need_ctx = l < depth - 1
        rows = R if need_ctx else lat_rows
        md = mods[l]
        w_in_p = _relayout_w_in(w_in[l])
        gbias = jnp.concatenate([a_gate_b[l].reshape(1, A_G), jnp.zeros((1, LANES - A_G), F32)], axis=1)
        wuq_p = jnp.pad(c_w_uq[l].reshape(C_Q_LORA, C_HEADS, C_NOPE + C_ROPE),
                        ((0, 0), (0, 0), (0, C_QPAD - C_NOPE - C_ROPE))).reshape(C_Q_LORA, C_HEADS * C_QPAD).astype(BF16)
        wr = jnp.concatenate([w_rg[l], w_re[l], jnp.zeros((D, LANES - N_GROUPS - n_experts), F32)], axis=1)
        wr_hi = wr.astype(BF16)
        wr_lo = (wr - wr_hi.astype(F32)).astype(BF16)
        br = jnp.concatenate([b_rg[l], b_re[l], jnp.zeros((LANES - N_GROUPS - n_experts,), F32)])[None, :]

        aqk, av, ag, aog, bq, bk, bv, cq, ckv, ckr = _in_proj(
            xa, md[0], md[1], w_in_p, cos_b, sin_b, b_qnorm_g[l][None, :], b_knorm_g[l][None, :], rowmod)
        qa, kt = _qk_conv(aqk, a_conv_w[l], a_conv_b[l][None, :], lat_rows, T, CL)
        hf, hb = _mlstm_scan(qa, kt, av, ag, gbias, Bn, T, CL)
        qm, km, vm = _mla_prep(cq, ckv, ckr, c_qnorm_g[l][None, :], c_kvnorm_g[l][None, :], wuq_p,
                               c_w_ukv[l].astype(BF16), cos_c, sin_c)
        geo = dict(n_batch=Bn, seq_len=T, ctx_len=CL, out_rows=rows)
        yb = _attention(bq, bk, bv, ctx_queries=False, tq=tq_b, tk=tk, **geo)
        ym = _attention(qm, km, vm, ctx_queries=False, tq=tq_c, tk=tk, **geo)
        if need_ctx:
            yb = _attention(bq, bk, bv, ctx_queries=True, tq=tq_b, tk=tk, prev_out=yb, **geo)
            ym = _attention(qm, km, vm, ctx_queries=True, tq=min(tq_c, CL), tk=tk, prev_out=ym, **geo)
        x1, h2, logits = _out_proj(
            xa, hf, hb, aog, a_norm_g[l][None, :], yb, ym, w_out[l].astype(BF16), md[2],
            ln1_g[l][None, :], ln1_b[l][None, :], md[3], md[4], wr_hi, wr_lo, br, rowmod, rows, alpha)

        src_tok, wts, tile_expert, n_valid, dest = _route(logits, tm, n_experts)
        xs = jnp.take(h2, src_tok, axis=0)
        ys = _moe(tile_expert, n_valid, xs, wts, e_gate, e_up, e_down, l)
        f = jnp.take(ys, dest[:, 0], axis=0) + jnp.take(ys, dest[:, 1], axis=0)
        xa = _ln2(x1, f, md[5], ln2_g[l][None, :], ln2_b[l][None, :], rowmod, rows, alpha)
    return xa[:lat_rows].reshape(Bn, T, D)
```
